```python
import math
import jax
import jax.numpy as jnp
from jax import lax
import numpy as np

D_MODEL = 1024
BATCH = 8
SEQ = 8192
DEPTH = 1
DEC_BATCH = 16
DEC_SEQ = 16
PAST_LEN = 2048

CHUNK = 64
PLE_DIM = 256
H_A = 8
DK = 64
DV = 64
W_A = H_A * DV
QKV_A = 2 * H_A * DK + H_A * DV
CONV_W = 4
H_B = 8
D_B = 64
W_B = H_B * D_B
LEFT_CHUNKS = 8
BAND_PAST = LEFT_CHUNKS * CHUNK
MAX_REL = 128
N_REL = 2 * MAX_REL + 1
D_MIX = W_A + W_B
SPLIT_SIZES = (QKV_A, H_A, H_A, W_A, W_B, W_B, W_B, W_B)
D_PROJ = sum(SPLIT_SIZES)
EPS = 1e-6

kernel_name = 'hybrid_gdn_chunkband_streaming_step'


def rms_norm(x, g):
    xf = x.astype(jnp.float32)
    y = xf * lax.rsqrt(jnp.mean(xf * xf, axis=-1, keepdims=True) + EPS)
    return (y * g.astype(jnp.float32)).astype(x.dtype)


def l2_norm(x):
    xf = x.astype(jnp.float32)
    return (xf * lax.rsqrt(jnp.sum(xf * xf, axis=-1, keepdims=True) + EPS)).astype(x.dtype)


def _gated_delta_blocked(q, k, v, g, beta, s0):
    b, t, h, _ = q.shape
    dv = v.shape[-1]
    out_dtype, st_dtype = v.dtype, s0.dtype
    blen = min(CHUNK, t)
    n = t // blen
    f32 = jnp.float32

    def blk(a):
        return a.astype(f32).reshape(b, n, blen, h, -1).transpose(0, 3, 1, 2, 4)

    q, k, v = blk(q), blk(k), blk(v)
    g = g.astype(f32).reshape(b, n, blen, h).transpose(0, 3, 1, 2)
    beta = beta.astype(f32).reshape(b, n, blen, h).transpose(0, 3, 1, 2)
    gc = jnp.cumsum(g, axis=-1)
    incl = jnp.tril(jnp.ones((blen, blen), bool))
    strict = jnp.tril(jnp.ones((blen, blen), bool), -1)
    decay = jnp.exp(jnp.where(incl, gc[..., :, None] - gc[..., None, :], -jnp.inf))
    kk = jnp.einsum('bhnid,bhnjd->bhnij', k, k)
    lower = jnp.where(strict, beta[..., :, None] * kk * decay, 0.0) + jnp.eye(blen, dtype=f32)
    rhs = jnp.concatenate([v * beta[..., None], k * (beta * jnp.exp(gc))[..., None]], axis=-1)
    sol = lax.linalg.triangular_solve(lower, rhs, left_side=True, lower=True)
    u, w = sol[..., :dv], sol[..., dv:]
    qk = jnp.einsum('bhnid,bhnjd->bhnij', q, k) * decay
    q_dec = q * jnp.exp(gc)[..., None]
    k_tail = k * jnp.exp(gc[..., -1:] - gc)[..., None]
    g_tot = jnp.exp(gc[..., -1])

    def step(s, xs):
        u_c, w_c, qk_c, qd_c, kt_c, gt_c = xs
        v_new = u_c - jnp.einsum('bhld,bhde->bhle', w_c, s)
        o_c = jnp.einsum('bhld,bhde->bhle', qd_c, s) + jnp.einsum('bhij,bhje->bhie', qk_c, v_new)
        s = s * gt_c[..., None, None] + jnp.einsum('bhld,bhle->bhde', kt_c, v_new)
        return s, o_c

    xs = tuple(jnp.moveaxis(a, 2, 0) for a in (u, w, qk, q_dec, k_tail, g_tot))
    s_fin, o = lax.scan(step, s0.astype(f32), xs)
    o = o.transpose(1, 0, 3, 2, 4).reshape(b, t, h, dv)
    return o.astype(out_dtype), s_fin.astype(st_dtype)


def _rel_bias(table, qpos, kpos):
    rel = jnp.clip(qpos[:, None] - kpos[None, :], -MAX_REL, MAX_REL) + MAX_REL
    return table[:, rel].astype(jnp.float32)


def _softmax_attend(q, k, v, bias, valid=None):
    s = jnp.einsum('bqhd,bkhd->bhqk', q, k).astype(jnp.float32) * (D_B ** -0.5) + bias
    if valid is not None:
        s = jnp.where(valid, s, -jnp.inf)
    p = jax.nn.softmax(s, axis=-1)
    return jnp.einsum('bhqk,bkhd->bqhd', p.astype(v.dtype), v)


def _band_attention_prompt(q, k, v, table):
    b, t, h, d = q.shape
    n = t // CHUNK
    span = BAND_PAST + CHUNK
    kpad = jnp.pad(k, ((0, 0), (BAND_PAST, 0), (0, 0), (0, 0)))
    vpad = jnp.pad(v, ((0, 0), (BAND_PAST, 0), (0, 0), (0, 0)))
    bias = _rel_bias(table, BAND_PAST + jnp.arange(CHUNK), jnp.arange(span))
    qc = jnp.moveaxis(q.reshape(b, n, CHUNK, h, d), 1, 0)

    def one(args):
        c, qb = args
        start = c * CHUNK
        kb = lax.dynamic_slice_in_dim(kpad, start, span, axis=1)
        vb = lax.dynamic_slice_in_dim(vpad, start, span, axis=1)
        valid = (start + jnp.arange(span)) >= BAND_PAST
        return _softmax_attend(qb, kb, vb, bias, valid)

    o = lax.map(one, (jnp.arange(n), qc))
    return jnp.moveaxis(o, 0, 1).reshape(b, t, h, d)


def _band_attention_sample(q, k_new, v_new, k_cache, v_cache, table):
    wc = k_cache.shape[1]
    tn = q.shape[1]
    kb = jnp.concatenate([k_cache, k_new], axis=1)
    vb = jnp.concatenate([v_cache, v_new], axis=1)
    bias = _rel_bias(table, wc + jnp.arange(tn), jnp.arange(wc + tn))
    return _softmax_attend(q, kb, vb, bias)


def _layer(h, p_i, conv0, s0, k_cache, v_cache, ln_g, w_in, conv_w, a_log, dt_bias,
           gdn_norm_g, q_norm_g, k_norm_g, rel_bias, w_out, w_ple_proj, ple_norm_g, w_ple_gate):
    b, t, _ = h.shape
    xn = rms_norm(h, ln_g)
    proj = xn @ w_in
    split_at = np.cumsum(SPLIT_SIZES)[:-1].tolist()
    qkv_a, a_raw, b_raw, z_a, q_b, k_b, v_b, z_b = jnp.split(proj, split_at, axis=-1)

    xcat = jnp.concatenate([conv0, qkv_a], axis=1)
    conv = sum(xcat[:, j:j + t] * conv_w[j] for j in range(CONV_W))
    new_conv = xcat[:, xcat.shape[1] - (CONV_W - 1):]
    conv = jax.nn.silu(conv)
    q_a, k_a, v_a = jnp.split(conv, [H_A * DK, 2 * H_A * DK], axis=-1)
    q_a = l2_norm(q_a.reshape(b, t, H_A, DK)) * (DK ** -0.5)
    k_a = l2_norm(k_a.reshape(b, t, H_A, DK))
    v_a = v_a.reshape(b, t, H_A, DV)
    beta = jax.nn.sigmoid(b_raw.astype(jnp.float32))
    g = -jnp.exp(a_log.astype(jnp.float32)) * jax.nn.softplus(
        a_raw.astype(jnp.float32) + dt_bias.astype(jnp.float32))
    o_a, s_new = _gated_delta_blocked(q_a, k_a, v_a, g, beta, s0)
    o_a = rms_norm(o_a, gdn_norm_g).reshape(b, t, W_A) * jax.nn.silu(z_a)

    q_b = rms_norm(q_b.reshape(b, t, H_B, D_B), q_norm_g)
    k_b = rms_norm(k_b.reshape(b, t, H_B, D_B), k_norm_g)
    v_b = v_b.reshape(b, t, H_B, D_B)
    if k_cache is None:
        o_b = _band_attention_prompt(q_b, k_b, v_b, rel_bias)
        keep = min(BAND_PAST, t)
        k_state, v_state = k_b[:, t - keep:], v_b[:, t - keep:]
    else:
        o_b = _band_attention_sample(q_b, k_b, v_b, k_cache, v_cache, rel_bias)
        k_state, v_state = k_b, v_b
    o_b = o_b.reshape(b, t, W_B) * jax.nn.silu(z_b)

    h = h + jnp.concatenate([o_a, o_b], axis=-1) @ w_out
    e = rms_norm(p_i @ w_ple_proj, ple_norm_g)
    h = h + jax.nn.sigmoid(h @ w_ple_gate) * e
    return h, new_conv, s_new, k_state, v_state


def setup_inputs(seed: int = 0) -> dict:
    key = jax.random.key(seed)
    ks = jax.random.split(key, 24)
    f32 = jnp.float32

    def nrm(k, shape, s):
        return jax.random.normal(k, shape, f32) * s

    wc = min(BAND_PAST, PAST_LEN)
    dt = jnp.exp(jax.random.uniform(ks[12], (DEPTH, H_A), f32, math.log(1e-3), math.log(1e-1)))
    return {
        'x_prompt': nrm(ks[0], (BATCH, SEQ, D_MODEL), 1.0),
        'x_sample': nrm(ks[1], (DEC_BATCH, DEC_SEQ, D_MODEL), 1.0),
        'state_conv': nrm(ks[2], (DEPTH, DEC_BATCH, CONV_W - 1, QKV_A), 1.0),
        'state_gdn': nrm(ks[3], (DEPTH, DEC_BATCH, H_A, DK, DV), 0.3),
        'cache_k': nrm(ks[4], (DEPTH, DEC_BATCH, wc, H_B, D_B), 1.0),
        'cache_v': nrm(ks[5], (DEPTH, DEC_BATCH, wc, H_B, D_B), 1.0),
        'p_prompt': nrm(ks[6], (DEPTH, BATCH, SEQ, PLE_DIM), 1.0),
        'p_sample': nrm(ks[7], (DEPTH, DEC_BATCH, DEC_SEQ, PLE_DIM), 1.0),
        'ln_g': 1.0 + nrm(ks[8], (DEPTH, D_MODEL), 0.05),
        'w_in': nrm(ks[9], (DEPTH, D_MODEL, D_PROJ), D_MODEL ** -0.5),
        'conv_w': nrm(ks[10], (DEPTH, CONV_W, QKV_A), CONV_W ** -0.5),
        'a_log': jnp.log(jax.random.uniform(ks[11], (DEPTH, H_A), f32, 1.0, 16.0)),
        'dt_bias': dt + jnp.log(-jnp.expm1(-dt)),
        'gdn_norm_g': 1.0 + nrm(ks[13], (DEPTH, DV), 0.05),
        'q_norm_g': 1.0 + nrm(ks[14], (DEPTH, D_B), 0.05),
        'k_norm_g': 1.0 + nrm(ks[15], (DEPTH, D_B), 0.05),
        'rel_bias': nrm(ks[16], (DEPTH, H_B, N_REL), 0.5),
        'w_out': nrm(ks[17], (DEPTH, D_MIX, D_MODEL), D_MIX ** -0.5),
        'w_ple_proj': nrm(ks[18], (DEPTH, PLE_DIM, D_MODEL), PLE_DIM ** -0.5),
        'ple_norm_g': 1.0 + nrm(ks[19], (DEPTH, D_MODEL), 0.05),
        'w_ple_gate': nrm(ks[20], (DEPTH, D_MODEL, D_MODEL), D_MODEL ** -0.5),
    }


def reference(x_prompt, x_sample, state_conv, state_gdn, cache_k, cache_v, p_prompt, p_sample,
              ln_g, w_in, conv_w, a_log, dt_bias, gdn_norm_g, q_norm_g, k_norm_g, rel_bias,
              w_out, w_ple_proj, ple_norm_g, w_ple_gate):
    hp, hs = x_prompt, x_sample
    bp = x_prompt.shape[0]
    conv_p, gdn_p, kp, vp = [], [], [], []
    conv_s, gdn_s, ksm, vsm = [], [], [], []
    for i in range(DEPTH):
        w_i = (ln_g[i], w_in[i], conv_w[i], a_log[i], dt_bias[i], gdn_norm_g[i], q_norm_g[i],
               k_norm_g[i], rel_bias[i], w_out[i], w_ple_proj[i], ple_norm_g[i], w_ple_gate[i])
        conv0 = jnp.zeros((bp, CONV_W - 1, QKV_A), x_prompt.dtype)
        s0 = jnp.zeros((bp, H_A, DK, DV), x_prompt.dtype)
        hp, c_p, g_p, k_p, v_p = _layer(hp, p_prompt[i], conv0, s0, None, None, *w_i)
        hs, c_s, g_s, k_s, v_s = _layer(hs, p_sample[i], state_conv[i], state_gdn[i],
                                        cache_k[i], cache_v[i], *w_i)
        conv_p.append(c_p); gdn_p.append(g_p); kp.append(k_p); vp.append(v_p)
        conv_s.append(c_s); gdn_s.append(g_s); ksm.append(k_s); vsm.append(v_s)
    conv_prompt = jnp.stack(conv_p)
    gdn_prompt = jnp.stack(gdn_p)
    k_prompt = jnp.stack(kp)
    v_prompt = jnp.stack(vp)
    conv_sample = jnp.stack(conv_s)
    gdn_sample = jnp.stack(gdn_s)
    k_sample = jnp.stack(ksm)
    v_sample = jnp.stack(vsm)
    return (hp, hs, conv_prompt, gdn_prompt, k_prompt, v_prompt,
            conv_sample, gdn_sample, k_sample, v_sample)
```

```python
import functools

import jax
import jax.numpy as jnp
from jax import lax
from jax.experimental import pallas as pl
from jax.experimental.pallas import tpu as pltpu

F32 = jnp.float32
BF16 = jnp.bfloat16
EPS = 1e-6

CHUNK = 64
N_HEADS = 8
D_HEAD = 64
W_GRP = N_HEADS * D_HEAD
QKV_A = 3 * W_GRP
CONV_W = 4
BAND_PAST = 8 * CHUNK
MAX_REL = 128
AB_PAD = 128
CONV_ROWS = 8
V7X_VMEM_LIMIT = 52 * 1024 * 1024

_OFF_ZA = QKV_A
_OFF_QB = _OFF_ZA + W_GRP
_OFF_KB = _OFF_QB + W_GRP
_OFF_VB = _OFF_KB + W_GRP
_OFF_ZB = _OFF_VB + W_GRP
_OFF_AB = _OFF_ZB + W_GRP
_D_PROJ_R = _OFF_AB + AB_PAD


def _dot(a, b):
    return jnp.dot(a.astype(BF16), b.astype(BF16), preferred_element_type=F32)


def _dot_nt(a, b):
    return lax.dot_general(a.astype(BF16), b.astype(BF16), (((1,), (1,)), ((), ())),
                           preferred_element_type=F32)


def _dot_tn(a, b):
    return lax.dot_general(a.astype(BF16), b.astype(BF16), (((0,), (0,)), ((), ())),
                           preferred_element_type=F32)


def _split3(x):
    hi = x.astype(BF16)
    r = x - hi.astype(F32)
    mid = r.astype(BF16)
    lo = (r - mid.astype(F32)).astype(BF16)
    return hi, mid, lo


def _dot_exact_lhs(a_bf16, x):
    hi, mid, lo = _split3(x)
    d = lambda y: jnp.dot(a_bf16, y, preferred_element_type=F32)
    return (d(lo) + d(mid)) + d(hi)


def _dot3(a, b):
    ah = a.astype(BF16)
    al = (a - ah.astype(F32)).astype(BF16)
    bh = b.astype(BF16)
    bl = (b - bh.astype(F32)).astype(BF16)
    d = lambda x, y: jnp.dot(x, y, preferred_element_type=F32)
    return (d(al, bh) + d(ah, bl)) + d(ah, bh)


def _sigmoid(x):
    return 1.0 / (1.0 + jnp.exp(-x))


def _silu(x):
    return x * _sigmoid(x)


def _softplus(x):
    return jnp.maximum(x, 0.0) + jnp.log1p(jnp.exp(-jnp.abs(x)))


def _head_mean_square(y, blockdiag_ones):
    return jnp.dot((y * y).astype(BF16), blockdiag_ones, preferred_element_type=F32) * (1.0 / D_HEAD)


def _inproj_kernel(x_ref, lng_ref, w_ref, qg_ref, kg_ref, bd_ref,
                   qkv_ref, ab_ref, ga_ref, q_ref, k_ref, v_ref, gb_ref, kst_ref, vst_ref):
    x = x_ref[0]
    xn = x * lax.rsqrt(jnp.mean(x * x, axis=-1, keepdims=True) + EPS) * lng_ref[...]
    xb = xn.astype(BF16)

    def proj(lo, hi):
        return jnp.dot(xb, w_ref[:, lo:hi], preferred_element_type=F32)

    bd = bd_ref[...]
    qkv_ref[0] = proj(0, QKV_A)
    ga_ref[0] = _silu(proj(_OFF_ZA, _OFF_QB))
    qb = proj(_OFF_QB, _OFF_KB)
    qn = qb * lax.rsqrt(_head_mean_square(qb, bd) + EPS) * qg_ref[...]
    q_ref[0] = (qn * (D_HEAD ** -0.5)).astype(BF16)
    kb = proj(_OFF_KB, _OFF_VB)
    kn = kb * lax.rsqrt(_head_mean_square(kb, bd) + EPS) * kg_ref[...]
    k_ref[0] = kn.astype(BF16)
    kst_ref[0] = kn
    vb = proj(_OFF_VB, _OFF_ZB)
    v_ref[0] = vb.astype(BF16)
    vst_ref[0] = vb
    gb_ref[0] = _silu(proj(_OFF_ZB, _OFF_AB))
    ab_ref[0] = proj(_OFF_AB, _D_PROJ_R)


def _inproj(x, ln_g, w_r, qg, kg, bd, tm):
    b, t, d = x.shape
    assert t % tm == 0
    grid = (b, t // tm)
    row = lambda w: pl.BlockSpec((1, tm, w), lambda i, j: (i, j, 0))
    const = lambda s: pl.BlockSpec(s, lambda i, j: (0,) * len(s))
    last = pl.BlockSpec((1, tm, W_GRP), lambda i, j: (i, 0, 0))
    sds = lambda w, dt: jax.ShapeDtypeStruct((b, t, w), dt)
    return pl.pallas_call(
        _inproj_kernel,
        grid=grid,
        in_specs=[row(d), const((1, d)), const((d, _D_PROJ_R)), const((1, W_GRP)), const((1, W_GRP)),
                  const((W_GRP, W_GRP))],
        out_specs=[row(QKV_A), row(AB_PAD), row(W_GRP), row(W_GRP), row(W_GRP), row(W_GRP), row(W_GRP),
                   last, last],
        out_shape=[sds(QKV_A, F32), sds(AB_PAD, F32), sds(W_GRP, F32), sds(W_GRP, BF16), sds(W_GRP, BF16),
                   sds(W_GRP, BF16), sds(W_GRP, F32),
                   jax.ShapeDtypeStruct((b, tm, W_GRP), F32), jax.ShapeDtypeStruct((b, tm, W_GRP), F32)],
        compiler_params=pltpu.CompilerParams(dimension_semantics=("arbitrary", "arbitrary"),
                                             vmem_limit_bytes=V7X_VMEM_LIMIT),
        name="inproj",
    )(x, ln_g, w_r, qg, kg, bd)


def _gdn_chunk(qc, kc, vc, g_c, beta_c, s_ref, ng, blen):
    ii = lax.broadcasted_iota(jnp.int32, (blen, blen), 0)
    jj = lax.broadcasted_iota(jnp.int32, (blen, blen), 1)
    incl = ii >= jj
    strict = ii > jj
    tri = jnp.where(incl, 1.0, 0.0).astype(BF16)
    upper = jnp.where(ii <= jj, 1.0, 0.0)
    ones = jnp.ones((blen, blen), BF16)
    eye = jnp.where(ii == jj, 1.0, 0.0)
    gc = _dot_exact_lhs(tri, g_c)
    outs = []
    for h in range(N_HEADS):
        sl = slice(h * D_HEAD, (h + 1) * D_HEAD)
        qh, kh, vh = qc[:, sl], kc[:, sl], vc[:, sl]
        g_col = g_c[:, h:h + 1]
        gc_col = gc[:, h:h + 1]
        b_col = beta_c[:, N_HEADS + h:N_HEADS + h + 1]
        gc_row = _dot_exact_lhs(ones, g_col * upper)
        decay = jnp.where(incl, jnp.exp(jnp.where(incl, gc_col - gc_row, 0.0)), 0.0)
        kk = _dot_nt(kh, kh)
        a_low = jnp.where(strict, b_col * kk * decay, 0.0)
        dinv = eye
        n = 1
        while n < blen:
            sh = n.bit_length() - 1
            in_pair = jnp.right_shift(ii, sh + 1) == jnp.right_shift(jj, sh + 1)
            off = jnp.where(in_pair & (jnp.right_shift(ii, sh) != jnp.right_shift(jj, sh)), a_low, 0.0)
            dinv = dinv - _dot3(_dot3(dinv, off), dinv)
            n *= 2
        egc = jnp.exp(gc_col)
        rhs = jnp.concatenate([vh * b_col, kh * (b_col * egc)], axis=1)
        sol = _dot3(dinv, rhs)
        u, w = sol[:, :D_HEAD], sol[:, D_HEAD:]
        qk = _dot_nt(qh, kh) * decay
        gc_last = gc[blen - 1:blen, h:h + 1]
        q_dec = qh * egc
        k_tail = kh * jnp.exp(gc_last - gc_col)
        g_tot = jnp.exp(gc_last)
        s = s_ref[h]
        v_new = u - _dot3(w, s)
        o_h = _dot(q_dec, s) + _dot(qk, v_new)
        s_ref[h] = s * g_tot + _dot_tn(k_tail, v_new)
        o_h = o_h * lax.rsqrt(jnp.mean(o_h * o_h, axis=-1, keepdims=True) + EPS) * ng
        outs.append(o_h)
    return jnp.concatenate(outs, axis=1)


def _gdn_kernel(qkv_ref, ab_ref, ga_ref, conv0_ref, s0_ref, cw_ref, alog_ref, dtb_ref, ng_ref, bd_ref,
                o_ref, convout_ref, sout_ref,
                xcat_ref, qn_ref, kn_ref, vn_ref, g_ref, beta_ref, s_ref, *, blen, tt):
    t = pl.program_id(1)

    @pl.when(t == 0)
    def _():
        xcat_ref[0:CONV_ROWS, :] = conv0_ref[0]
        s_ref[...] = s0_ref[0]

    xcat_ref[CONV_ROWS:CONV_ROWS + tt, :] = qkv_ref[0]
    cw = cw_ref[...]
    first = CONV_ROWS - (CONV_W - 1)
    conv = xcat_ref[first:first + tt, :] * cw[0:1, :]
    for j in range(1, CONV_W):
        conv = conv + xcat_ref[first + j:first + j + tt, :] * cw[j:j + 1, :]
    carry = xcat_ref[tt:tt + CONV_ROWS, :]
    convout_ref[0] = carry
    xcat_ref[0:CONV_ROWS, :] = carry

    c = _silu(conv)
    bd = bd_ref[...]
    q = c[:, 0:W_GRP]
    k = c[:, W_GRP:2 * W_GRP]
    qn_ref[...] = q * lax.rsqrt(_head_mean_square(q, bd) * D_HEAD + EPS) * (D_HEAD ** -0.5)
    kn_ref[...] = k * lax.rsqrt(_head_mean_square(k, bd) * D_HEAD + EPS)
    vn_ref[...] = c[:, 2 * W_GRP:3 * W_GRP]
    ab = ab_ref[0]
    g_ref[...] = -jnp.exp(alog_ref[...]) * _softplus(ab + dtb_ref[...])
    beta_ref[...] = _sigmoid(ab)
    ng = ng_ref[...]

    def chunk(ci, carry_):
        rows = pl.ds(pl.multiple_of(ci * blen, blen), blen)
        o = _gdn_chunk(qn_ref[rows, :], kn_ref[rows, :], vn_ref[rows, :], g_ref[rows, :], beta_ref[rows, :],
                       s_ref, ng, blen)
        o_ref[0, rows, :] = o * ga_ref[0, rows, :]
        return carry_

    lax.fori_loop(0, tt // blen, chunk, 0)

    @pl.when(t == pl.num_programs(1) - 1)
    def _():
        sout_ref[0] = s_ref[...]


def _gdn(qkv, ab, ga, conv0, s0, cw, alog, dtb, ng, bd, blen, tt):
    b, t, _ = qkv.shape
    assert t % tt == 0 and tt % blen == 0 and tt >= CONV_ROWS
    row = lambda w: pl.BlockSpec((1, tt, w), lambda i, j: (i, j, 0))
    const = lambda s: pl.BlockSpec(s, lambda i, j: (0,) * len(s))
    per_b = lambda s: pl.BlockSpec((1,) + s, lambda i, j: (i,) + (0,) * len(s))
    return pl.pallas_call(
        functools.partial(_gdn_kernel, blen=blen, tt=tt),
        grid=(b, t // tt),
        in_specs=[row(QKV_A), row(AB_PAD), row(W_GRP), per_b((CONV_ROWS, QKV_A)),
                  per_b((N_HEADS, D_HEAD, D_HEAD)), const((CONV_W, QKV_A)), const((1, AB_PAD)),
                  const((1, AB_PAD)), const((1, D_HEAD)), const((W_GRP, W_GRP))],
        out_specs=[row(W_GRP), per_b((CONV_ROWS, QKV_A)), per_b((N_HEADS, D_HEAD, D_HEAD))],
        out_shape=[jax.ShapeDtypeStruct((b, t, W_GRP), F32),
                   jax.ShapeDtypeStruct((b, CONV_ROWS, QKV_A), F32),
                   jax.ShapeDtypeStruct((b, N_HEADS, D_HEAD, D_HEAD), F32)],
        scratch_shapes=[pltpu.VMEM((tt + CONV_ROWS, QKV_A), F32),
                        pltpu.VMEM((tt, W_GRP), F32), pltpu.VMEM((tt, W_GRP), F32),
                        pltpu.VMEM((tt, W_GRP), F32), pltpu.VMEM((tt, AB_PAD), F32),
                        pltpu.VMEM((tt, AB_PAD), F32), pltpu.VMEM((N_HEADS, D_HEAD, D_HEAD), F32)],
        compiler_params=pltpu.CompilerParams(dimension_semantics=("arbitrary", "arbitrary"),
                                             vmem_limit_bytes=V7X_VMEM_LIMIT),
        name="gdn",
    )(qkv, ab, ga, conv0, s0, cw, alog, dtb, ng, bd)


def _attend_heads(q, kw, vw, bias_ref, first_valid):
    lq, span = q.shape[0], kw.shape[0]
    col = lax.broadcasted_iota(jnp.int32, (lq, span), 1)
    valid = col >= first_valid
    outs = []
    for h in range(N_HEADS):
        sl = slice(h * D_HEAD, (h + 1) * D_HEAD)
        s = lax.dot_general(q[:, sl], kw[:, sl], (((1,), (1,)), ((), ())), preferred_element_type=F32)
        s = jnp.where(valid, s + bias_ref[h], -jnp.inf)
        m = jnp.max(s, axis=-1, keepdims=True)
        p = jnp.exp(s - m)
        l = jnp.sum(p, axis=-1, keepdims=True)
        o = jnp.dot(p.astype(BF16), vw[:, sl], preferred_element_type=F32)
        outs.append(o / l)
    return jnp.concatenate(outs, axis=1)


def _attn_prompt_kernel(q_ref, kp_ref, kc_ref, vp_ref, vc_ref, gb_ref, bias_ref, o_ref, kwin_ref, vwin_ref):
    j = pl.program_id(1)
    tq = BAND_PAST
    kwin_ref[0:tq, :] = kp_ref[0]
    kwin_ref[tq:2 * tq, :] = kc_ref[0]
    vwin_ref[0:tq, :] = vp_ref[0]
    vwin_ref[tq:2 * tq, :] = vc_ref[0]

    def chunk(c, carry):
        r0 = pl.multiple_of(c * CHUNK, CHUNK)
        rows = pl.ds(r0, CHUNK)
        win = pl.ds(r0, BAND_PAST + CHUNK)
        first_valid = jnp.where(j == 0, BAND_PAST - c * CHUNK, 0)
        o = _attend_heads(q_ref[0, rows, :], kwin_ref[win, :], vwin_ref[win, :], bias_ref, first_valid)
        o_ref[0, rows, :] = o * gb_ref[0, rows, :]
        return carry

    lax.fori_loop(0, tq // CHUNK, chunk, 0)


def _attn_prompt(q, k, v, gb, bias):
    b, t, _ = q.shape
    tq = BAND_PAST
    assert t % tq == 0
    cur = pl.BlockSpec((1, tq, W_GRP), lambda i, j: (i, j, 0))
    prev = pl.BlockSpec((1, tq, W_GRP), lambda i, j: (i, jnp.maximum(j - 1, 0), 0))
    span = BAND_PAST + CHUNK
    return pl.pallas_call(
        _attn_prompt_kernel,
        grid=(b, t // tq),
        in_specs=[cur, prev, cur, prev, cur, cur,
                  pl.BlockSpec((N_HEADS, CHUNK, span), lambda i, j: (0, 0, 0))],
        out_specs=cur,
        out_shape=jax.ShapeDtypeStruct((b, t, W_GRP), F32),
        scratch_shapes=[pltpu.VMEM((2 * tq, W_GRP), BF16), pltpu.VMEM((2 * tq, W_GRP), BF16)],
        compiler_params=pltpu.CompilerParams(dimension_semantics=("arbitrary", "arbitrary"),
                                             vmem_limit_bytes=V7X_VMEM_LIMIT),
        name="attn_prompt",
    )(q, k, k, v, v, gb, bias)


def _attn_sample_kernel(q_ref, kn_ref, vn_ref, ck_ref, cv_ref, gb_ref, bias_ref, o_ref):
    kw = jnp.concatenate([ck_ref[0].astype(BF16), kn_ref[0]], axis=0)
    vw = jnp.concatenate([cv_ref[0].astype(BF16), vn_ref[0]], axis=0)
    o = _attend_heads(q_ref[0], kw, vw, bias_ref, 0)
    o_ref[0] = o * gb_ref[0]


def _attn_sample(q, k, v, ck, cv, gb, bias):
    b, tn, _ = q.shape
    wc = ck.shape[1]
    new = pl.BlockSpec((1, tn, W_GRP), lambda i: (i, 0, 0))
    old = pl.BlockSpec((1, wc, W_GRP), lambda i: (i, 0, 0))
    return pl.pallas_call(
        _attn_sample_kernel,
        grid=(b,),
        in_specs=[new, new, new, old, old, new, pl.BlockSpec((N_HEADS, tn, wc + tn), lambda i: (0, 0, 0))],
        out_specs=new,
        out_shape=jax.ShapeDtypeStruct((b, tn, W_GRP), F32),
        compiler_params=pltpu.CompilerParams(dimension_semantics=("arbitrary",),
                                             vmem_limit_bytes=V7X_VMEM_LIMIT),
        name="attn_sample",
    )(q, k, v, ck, cv, gb, bias)


def _outproj_kernel(x_ref, oa_ref, ob_ref, p_ref, wo_ref, wp_ref, pg_ref, wg_ref, y_ref):
    h = x_ref[0] + (_dot(oa_ref[0], wo_ref[0:W_GRP, :]) + _dot(ob_ref[0], wo_ref[W_GRP:2 * W_GRP, :]))
    e = _dot(p_ref[0], wp_ref[...])
    e = e * lax.rsqrt(jnp.mean(e * e, axis=-1, keepdims=True) + EPS) * pg_ref[...]
    y_ref[0] = h + _sigmoid(_dot(h, wg_ref[...])) * e


def _outproj(x, oa, ob, p, wo, wp, pg, wg, tm):
    b, t, d = x.shape
    pd = p.shape[-1]
    assert t % tm == 0
    row = lambda w: pl.BlockSpec((1, tm, w), lambda i, j: (i, j, 0))
    const = lambda s: pl.BlockSpec(s, lambda i, j: (0,) * len(s))
    return pl.pallas_call(
        _outproj_kernel,
        grid=(b, t // tm),
        in_specs=[row(d), row(W_GRP), row(W_GRP), row(pd), const((2 * W_GRP, d)), const((pd, d)),
                  const((1, d)), const((d, d))],
        out_specs=row(d),
        out_shape=jax.ShapeDtypeStruct((b, t, d), F32),
        compiler_params=pltpu.CompilerParams(dimension_semantics=("arbitrary", "arbitrary"),
                                             vmem_limit_bytes=V7X_VMEM_LIMIT),
        name="outproj",
    )(x, oa, ob, p, wo, wp, pg, wg)


def _rel_bias_table(table, qpos, kpos):
    rel = jnp.clip(qpos[:, None] - kpos[None, :], -MAX_REL, MAX_REL) + MAX_REL
    return table[:, rel].astype(F32)


def _pad_lanes(v, width):
    return jnp.pad(v.reshape(1, -1), ((0, 0), (0, width - v.shape[-1])))


def _layer_weights(ln_g, w_in, conv_w, a_log, dt_bias, gdn_norm_g, q_norm_g, k_norm_g, w_out, w_ple_proj,
                   ple_norm_g, w_ple_gate):
    ab0 = QKV_A
    ab1 = QKV_A + 2 * N_HEADS
    w_r = jnp.concatenate([w_in[:, :ab0], w_in[:, ab1:],
                           jnp.pad(w_in[:, ab0:ab1], ((0, 0), (0, AB_PAD - 2 * N_HEADS)))], axis=1)
    head = jnp.arange(W_GRP) // D_HEAD
    return dict(
        ln_g=ln_g.reshape(1, -1), w_r=w_r.astype(BF16),
        qg=jnp.tile(q_norm_g, N_HEADS).reshape(1, -1), kg=jnp.tile(k_norm_g, N_HEADS).reshape(1, -1),
        bd=(head[:, None] == head[None, :]).astype(BF16),
        cw=conv_w, alog=_pad_lanes(a_log, AB_PAD), dtb=_pad_lanes(dt_bias, AB_PAD),
        ng=gdn_norm_g.reshape(1, -1),
        wo=w_out.astype(BF16), wp=w_ple_proj.astype(BF16), pg=ple_norm_g.reshape(1, -1),
        wg=w_ple_gate.astype(BF16))


def _layer(h, p_i, conv0, s0, caches, bias, w, tm, blen, tt):
    qkv, ab, ga, q, k, v, gb, kst, vst = _inproj(h, w["ln_g"], w["w_r"], w["qg"], w["kg"], w["bd"], tm)
    if caches is not None:
        nb, tn = caches[0].shape[0], h.shape[1] // caches[0].shape[0]
        regroup = lambda a: a.reshape(nb, tn, a.shape[-1])
        qkv, ab, ga, q, k, v, gb = map(regroup, (qkv, ab, ga, q, k, v, gb))
    oa, conv_new, s_new = _gdn(qkv, ab, ga, conv0, s0, w["cw"], w["alog"], w["dtb"], w["ng"], w["bd"],
                               blen, tt)
    if caches is None:
        ob = _attn_prompt(q, k, v, gb, bias)
    else:
        ob = _attn_sample(q, k, v, caches[0], caches[1], gb, bias)
        oa, ob = oa.reshape(h.shape[0], h.shape[1], W_GRP), ob.reshape(h.shape[0], h.shape[1], W_GRP)
    y = _outproj(h, oa, ob, p_i, w["wo"], w["wp"], w["pg"], w["wg"], tm)
    return y, conv_new[:, CONV_ROWS - (CONV_W - 1):], s_new, kst, vst


def kernel(x_prompt, x_sample, state_conv, state_gdn, cache_k, cache_v, p_prompt, p_sample, ln_g, w_in, conv_w, a_log, dt_bias, gdn_norm_g, q_norm_g, k_norm_g, rel_bias, w_out, w_ple_proj, ple_norm_g, w_ple_gate):
    depth = ln_g.shape[0]
    bp, tp, d = x_prompt.shape
    bs, ts, _ = x_sample.shape
    wc = cache_k.shape[2]
    assert tp % BAND_PAST == 0 and ts <= CHUNK and ts % CONV_ROWS == 0
    hp = x_prompt
    hs = x_sample.reshape(1, bs * ts, d)
    bias_p = lambda tab: _rel_bias_table(tab, BAND_PAST + jnp.arange(CHUNK), jnp.arange(BAND_PAST + CHUNK))
    bias_s = lambda tab: _rel_bias_table(tab, wc + jnp.arange(ts), jnp.arange(wc + ts))
    outs = [[] for _ in range(8)]
    for i in range(depth):
        w = _layer_weights(ln_g[i], w_in[i], conv_w[i], a_log[i], dt_bias[i], gdn_norm_g[i], q_norm_g[i],
                           k_norm_g[i], w_out[i], w_ple_proj[i], ple_norm_g[i], w_ple_gate[i])
        conv0 = jnp.zeros((bp, CONV_ROWS, QKV_A), F32)
        s0 = jnp.zeros((bp, N_HEADS, D_HEAD, D_HEAD), F32)
        hp, c_p, g_p, k_p, v_p = _layer(hp, p_prompt[i], conv0, s0, None, bias_p(rel_bias[i]), w,
                                        tm=BAND_PAST, blen=CHUNK, tt=4 * CHUNK)
        conv0_s = jnp.pad(state_conv[i], ((0, 0), (CONV_ROWS - (CONV_W - 1), 0), (0, 0)))
        caches = (cache_k[i].reshape(bs, wc, W_GRP), cache_v[i].reshape(bs, wc, W_GRP))
        hs, c_s, g_s, k_s, v_s = _layer(hs, p_sample[i].reshape(1, bs * ts, -1), conv0_s, state_gdn[i],
                                        caches, bias_s(rel_bias[i]), w, tm=bs * ts, blen=ts, tt=ts)
        new = (c_p, g_p, k_p.reshape(bp, BAND_PAST, N_HEADS, D_HEAD), v_p.reshape(bp, BAND_PAST, N_HEADS, D_HEAD),
               c_s, g_s, k_s.reshape(bs, ts, N_HEADS, D_HEAD), v_s.reshape(bs, ts, N_HEADS, D_HEAD))
        for lst, a in zip(outs, new):
            lst.append(a)
    return (hp, hs.reshape(bs, ts, d)) + tuple(jnp.stack(lst) for lst in outs)
```

```python
import functools

import numpy as np

import jax
import jax.numpy as jnp
from jax import lax
from jax.experimental import pallas as pl
from jax.experimental.pallas import tpu as pltpu

F32 = jnp.float32
BF16 = jnp.bfloat16
EPS = 1e-6

CHUNK = 64
N_HEADS = 8
D_HEAD = 64
W_GRP = N_HEADS * D_HEAD
QKV_A = 3 * W_GRP
CONV_W = 4
BAND_PAST = 8 * CHUNK
MAX_REL = 128
AB_PAD = 128
CONV_ROWS = 8
V7X_VMEM_LIMIT = 52 * 1024 * 1024
MXU_W = 256
MXU_HEADS = MXU_W // D_HEAD
N_DOUBLINGS = 5

_OFF_ZA = QKV_A
_OFF_QB = _OFF_ZA + W_GRP
_OFF_KB = _OFF_QB + W_GRP
_OFF_VB = _OFF_KB + W_GRP
_OFF_ZB = _OFF_VB + W_GRP
_OFF_AB = _OFF_ZB + W_GRP
_D_PROJ_R = _OFF_AB + AB_PAD


def _dot(a, b):
    return jnp.dot(a.astype(BF16), b.astype(BF16), preferred_element_type=F32)


def _dot_split(x, w_bf16, n_terms):
    terms, r = [], x
    for _ in range(n_terms):
        t = r.astype(BF16)
        terms.append(t)
        r = r - t.astype(F32)
    acc = None
    for t in reversed(terms):
        d = jnp.dot(t, w_bf16, preferred_element_type=F32)
        acc = d if acc is None else acc + d
    return acc


def _sigmoid(x):
    return 1.0 / (1.0 + jnp.exp(-x))


def _silu(x):
    return x * _sigmoid(x)


def _softplus(x):
    return jnp.maximum(x, 0.0) + jnp.log1p(jnp.exp(-jnp.abs(x)))


def _head_mean_square(y, blockdiag_ones):
    return jnp.dot((y * y).astype(BF16), blockdiag_ones, preferred_element_type=F32) * (1.0 / D_HEAD)


def _inproj_kernel(x_ref, lng_ref, w_ref, qg_ref, kg_ref, bd_ref,
                   qkv_ref, ab_ref, ga_ref, q_ref, k_ref, v_ref, gb_ref, kst_ref, vst_ref):
    x = x_ref[0]
    xn = x * lax.rsqrt(jnp.mean(x * x, axis=-1, keepdims=True) + EPS) * lng_ref[...]
    xb = xn.astype(BF16)

    def proj(lo, hi):
        return jnp.dot(xb, w_ref[:, lo:hi], preferred_element_type=F32)

    bd = bd_ref[...]
    qkv_ref[0] = proj(0, QKV_A)
    ga_ref[0] = _silu(proj(_OFF_ZA, _OFF_QB))
    qb = proj(_OFF_QB, _OFF_KB)
    qn = qb * lax.rsqrt(_head_mean_square(qb, bd) + EPS) * qg_ref[...]
    q_ref[0] = (qn * (D_HEAD ** -0.5)).astype(BF16)
    kb = proj(_OFF_KB, _OFF_VB)
    kn = kb * lax.rsqrt(_head_mean_square(kb, bd) + EPS) * kg_ref[...]
    k_ref[0] = kn.astype(BF16)
    kst_ref[0] = kn
    vb = proj(_OFF_VB, _OFF_ZB)
    v_ref[0] = vb.astype(BF16)
    vst_ref[0] = vb
    gb_ref[0] = _silu(proj(_OFF_ZB, _OFF_AB))
    ab_ref[0] = proj(_OFF_AB, _D_PROJ_R)


def _inproj(x, ln_g, w_r, qg, kg, bd, tm):
    b, t, d = x.shape
    assert t % tm == 0
    grid = (b, t // tm)
    row = lambda w: pl.BlockSpec((1, tm, w), lambda i, j: (i, j, 0))
    const = lambda s: pl.BlockSpec(s, lambda i, j: (0,) * len(s))
    last = pl.BlockSpec((1, tm, W_GRP), lambda i, j: (i, 0, 0))
    sds = lambda w, dt: jax.ShapeDtypeStruct((b, t, w), dt)
    return pl.pallas_call(
        _inproj_kernel,
        grid=grid,
        in_specs=[row(d), const((1, d)), const((d, _D_PROJ_R)), const((1, W_GRP)), const((1, W_GRP)),
                  const((W_GRP, W_GRP))],
        out_specs=[row(QKV_A), row(AB_PAD), row(W_GRP), row(W_GRP), row(W_GRP), row(W_GRP), row(W_GRP),
                   last, last],
        out_shape=[sds(QKV_A, F32), sds(AB_PAD, F32), sds(W_GRP, F32), sds(W_GRP, BF16), sds(W_GRP, BF16),
                   sds(W_GRP, BF16), sds(W_GRP, F32),
                   jax.ShapeDtypeStruct((b, tm, W_GRP), F32), jax.ShapeDtypeStruct((b, tm, W_GRP), F32)],
        compiler_params=pltpu.CompilerParams(dimension_semantics=("arbitrary", "arbitrary"),
                                             vmem_limit_bytes=V7X_VMEM_LIMIT),
        name="inproj",
    )(x, ln_g, w_r, qg, kg, bd)


def _tile_rows(x, n):
    return jnp.concatenate([x] * n, axis=0)


def _gdn_tile(q_s, k_s, v_s, gcx_s, bex_s, o_s, s_ref, bdm_ref, bdmf_ref, lvl_ref, n_chunks):
    row = lax.broadcasted_iota(jnp.int32, (CHUNK, MXU_W), 0)
    col = jnp.bitwise_and(lax.broadcasted_iota(jnp.int32, (CHUNK, MXU_W), 1), D_HEAD - 1)
    incl = row >= col
    strict = row > col
    diag = row == col
    pair = strict & (jnp.right_shift(row, 1) == jnp.right_shift(col, 1))
    bdm = bdm_ref[...]
    blockdiag = lambda x: _tile_rows(x.astype(BF16), MXU_HEADS) * bdm
    mm = lambda a, b: jnp.dot(a, b, preferred_element_type=F32)

    chains = []
    for c in range(n_chunks):
        rows = slice(c * CHUNK, (c + 1) * CHUNK)
        for grp in range(N_HEADS // MXU_HEADS):
            ls = slice(grp * MXU_W, (grp + 1) * MXU_W)
            qc, kc, vc, gcx, bex = (r[rows, ls] for r in (q_s, k_s, v_s, gcx_s, bex_s))
            gc_row = jnp.sum(jnp.where(diag, gcx, 0.0), axis=0, keepdims=True)
            gc_last = gcx[CHUNK - 1:CHUNK, :]
            egc = jnp.exp(gcx)
            k16 = kc.astype(BF16)
            chains.append(dict(
                rows=rows, ls=ls, bex=bex, k16=k16,
                decay=jnp.where(incl, jnp.exp(jnp.where(incl, gcx - gc_row, 0.0)), 0.0),
                g_tot=jnp.exp(gc_last), u_rhs=vc * bex, w_rhs=kc * (bex * egc),
                qk_lhs=jnp.concatenate([qc.astype(BF16), k16], axis=0),
                q_dec=(qc * egc).astype(BF16), k_tail=(kc * jnp.exp(gc_last - gcx)).astype(BF16)))
    for ch in chains:
        ch["sc"] = lax.dot_general(ch["qk_lhs"], blockdiag(ch["k16"]), (((1,), (1,)), ((), ())),
                                   preferred_element_type=F32)
    for ch in chains:
        sc = ch.pop("sc")
        ch["qk"] = (sc[0:CHUNK] * ch["decay"]).astype(BF16)
        a_low = jnp.where(strict, ch["bex"] * sc[CHUNK:2 * CHUNK] * ch["decay"], 0.0)
        ch["a_rows"] = _tile_rows(a_low.astype(BF16), MXU_HEADS)
        ch["dinv"] = jnp.where(diag, 1.0, 0.0) - jnp.where(pair, a_low, 0.0)
    for li in range(N_DOUBLINGS):
        for ch in chains:
            ch["t1"] = mm(ch["dinv"].astype(BF16), ch["a_rows"] * lvl_ref[li])
        for ch in chains:
            ch["dinv"] = ch["dinv"] - mm(ch.pop("t1").astype(BF16), blockdiag(ch["dinv"]))
    for ch in chains:
        d16 = ch["dinv"].astype(BF16)
        ch["u"] = mm(d16, blockdiag(ch["u_rhs"]))
        ch["w"] = mm(d16, blockdiag(ch["w_rhs"]))
    for ch in chains:
        rows, ls = ch["rows"], ch["ls"]
        s = s_ref[:, ls]
        wq = mm(jnp.concatenate([ch["w"].astype(BF16), ch["q_dec"]], axis=0), blockdiag(s))
        v16 = (ch["u"] - wq[0:CHUNK]).astype(BF16)
        o_s[rows, ls] = wq[CHUNK:2 * CHUNK] + mm(ch["qk"], _tile_rows(v16, MXU_HEADS) * bdm)
        r = lax.dot_general(ch["k_tail"], v16, (((0,), (0,)), ((), ())),
                            preferred_element_type=F32) * bdmf_ref[...]
        s_ref[:, ls] = s * ch["g_tot"] + ((r[0:D_HEAD] + r[D_HEAD:2 * D_HEAD])
                                          + (r[2 * D_HEAD:3 * D_HEAD] + r[3 * D_HEAD:4 * D_HEAD]))


def _gdn_kernel(qkv_ref, ab_ref, ga_ref, conv0_ref, s0_ref, cw_ref, alog_ref, dtb_ref, ng_ref, bdones_ref,
                ea_ref, eb_ref, bdm_ref, bdmf_ref, lvl_ref,
                o_ref, convout_ref, sout_ref,
                xcat_ref, q_s, k_s, v_s, gcx_s, bex_s, o_s, s_ref, *, tt_in, tt):
    t = pl.program_id(1)

    @pl.when(t == 0)
    def _():
        xcat_ref[0:CONV_ROWS, :] = conv0_ref[0]
        s_ref[...] = s0_ref[0]

    xcat_ref[CONV_ROWS:CONV_ROWS + tt_in, :] = qkv_ref[0]
    if tt > tt_in:
        xcat_ref[CONV_ROWS + tt_in:CONV_ROWS + tt, :] = jnp.zeros((tt - tt_in, QKV_A), F32)
    cw = cw_ref[...]
    first = CONV_ROWS - (CONV_W - 1)
    conv = xcat_ref[first:first + tt, :] * cw[0:1, :]
    for j in range(1, CONV_W):
        conv = conv + xcat_ref[first + j:first + j + tt, :] * cw[j:j + 1, :]
    carry = xcat_ref[tt_in:tt_in + CONV_ROWS, :]
    convout_ref[0] = carry
    xcat_ref[0:CONV_ROWS, :] = carry

    c = _silu(conv)
    bdones = bdones_ref[...]
    q = c[:, 0:W_GRP]
    k = c[:, W_GRP:2 * W_GRP]
    q_s[...] = q * lax.rsqrt(_head_mean_square(q, bdones) * D_HEAD + EPS) * (D_HEAD ** -0.5)
    k_s[...] = k * lax.rsqrt(_head_mean_square(k, bdones) * D_HEAD + EPS)
    v_s[...] = c[:, 2 * W_GRP:3 * W_GRP]

    ab = ab_ref[0]
    g = -jnp.exp(alog_ref[...]) * _softplus(ab + dtb_ref[...])
    beta = _sigmoid(ab)
    if tt > tt_in:
        pad = jnp.zeros((tt - tt_in, AB_PAD), F32)
        g = jnp.concatenate([g, pad], axis=0)
        beta = jnp.concatenate([beta, pad], axis=0)
    rin = jnp.bitwise_and(lax.broadcasted_iota(jnp.int32, (tt, AB_PAD), 0), CHUNK - 1)
    gc = g
    step = 1
    while step < CHUNK:
        gc = gc + jnp.where(rin >= step, pltpu.roll(gc, step, 0), 0.0)
        step *= 2
    gcx_s[...] = _dot_split(gc, ea_ref[...], 3)
    bex_s[...] = _dot_split(beta, eb_ref[...], 2)

    _gdn_tile(q_s, k_s, v_s, gcx_s, bex_s, o_s, s_ref, bdm_ref, bdmf_ref, lvl_ref, tt // CHUNK)

    o = o_s[0:tt_in, :]
    o_ref[0] = o * lax.rsqrt(_head_mean_square(o, bdones) + EPS) * ng_ref[...] * ga_ref[0]

    @pl.when(t == pl.num_programs(1) - 1)
    def _():
        sout_ref[0] = s_ref[...]


def _gdn_masks():
    idx = np.arange(MXU_W)
    head, pos = idx // D_HEAD, idx % D_HEAD
    bdm = head[:, None] == head[None, :]
    levels = []
    for li in range(N_DOUBLINGS):
        n = 2 << li
        pr, pc = pos[:, None], pos[None, :]
        levels.append(bdm & (pr // (2 * n) == pc // (2 * n)) & (pr // n != pc // n))
    expand = np.arange(W_GRP)[None, :] // D_HEAD == np.arange(AB_PAD)[:, None]
    return dict(bdm=jnp.asarray(bdm, BF16), bdmf=jnp.asarray(bdm, F32),
                lvl=jnp.asarray(np.stack(levels), BF16),
                ea=jnp.asarray(expand, BF16), eb=jnp.asarray(np.roll(expand, N_HEADS, axis=0), BF16))


def _gdn(qkv, ab, ga, conv0, s0, cw, alog, dtb, ng, bdones, tt_in):
    b, t, _ = qkv.shape
    tt = -(-tt_in // CHUNK) * CHUNK
    assert t % tt_in == 0 and tt_in >= CONV_ROWS and (tt == tt_in or t == tt_in)
    m = _gdn_masks()
    row = lambda w: pl.BlockSpec((1, tt_in, w), lambda i, j: (i, j, 0))
    const = lambda s: pl.BlockSpec(s, lambda i, j: (0,) * len(s))
    per_b = lambda s: pl.BlockSpec((1,) + s, lambda i, j: (i,) + (0,) * len(s))
    return pl.pallas_call(
        functools.partial(_gdn_kernel, tt_in=tt_in, tt=tt),
        grid=(b, t // tt_in),
        in_specs=[row(QKV_A), row(AB_PAD), row(W_GRP), per_b((CONV_ROWS, QKV_A)), per_b((D_HEAD, W_GRP)),
                  const((CONV_W, QKV_A)), const((1, AB_PAD)), const((1, AB_PAD)), const((1, W_GRP)),
                  const((W_GRP, W_GRP)), const((AB_PAD, W_GRP)), const((AB_PAD, W_GRP)),
                  const((MXU_W, MXU_W)), const((MXU_W, MXU_W)), const((N_DOUBLINGS, MXU_W, MXU_W))],
        out_specs=[row(W_GRP), per_b((CONV_ROWS, QKV_A)), per_b((D_HEAD, W_GRP))],
        out_shape=[jax.ShapeDtypeStruct((b, t, W_GRP), F32),
                   jax.ShapeDtypeStruct((b, CONV_ROWS, QKV_A), F32),
                   jax.ShapeDtypeStruct((b, D_HEAD, W_GRP), F32)],
        scratch_shapes=[pltpu.VMEM((tt + CONV_ROWS, QKV_A), F32)]
        + [pltpu.VMEM((tt, W_GRP), F32)] * 6 + [pltpu.VMEM((D_HEAD, W_GRP), F32)],
        compiler_params=pltpu.CompilerParams(dimension_semantics=("arbitrary", "arbitrary"),
                                             vmem_limit_bytes=V7X_VMEM_LIMIT),
        name="gdn",
    )(qkv, ab, ga, conv0, s0, cw, alog, dtb, ng, bdones, m["ea"], m["eb"], m["bdm"], m["bdmf"], m["lvl"])


def _attend_heads(q, kw, vw, bias_ref, first_valid):
    lq, span = q.shape[0], kw.shape[0]
    col = lax.broadcasted_iota(jnp.int32, (lq, span), 1)
    valid = col >= first_valid
    outs = []
    for h in range(N_HEADS):
        sl = slice(h * D_HEAD, (h + 1) * D_HEAD)
        s = lax.dot_general(q[:, sl], kw[:, sl], (((1,), (1,)), ((), ())), preferred_element_type=F32)
        s = jnp.where(valid, s + bias_ref[h], -jnp.inf)
        m = jnp.max(s, axis=-1, keepdims=True)
        p = jnp.exp(s - m)
        l = jnp.sum(p, axis=-1, keepdims=True)
        o = jnp.dot(p.astype(BF16), vw[:, sl], preferred_element_type=F32)
        outs.append(o / l)
    return jnp.concatenate(outs, axis=1)


def _attn_prompt_kernel(q_ref, kp_ref, kc_ref, vp_ref, vc_ref, gb_ref, bias_ref, o_ref, kwin_ref, vwin_ref):
    j = pl.program_id(1)
    tq = BAND_PAST
    kwin_ref[0:tq, :] = kp_ref[0]
    kwin_ref[tq:2 * tq, :] = kc_ref[0]
    vwin_ref[0:tq, :] = vp_ref[0]
    vwin_ref[tq:2 * tq, :] = vc_ref[0]

    def chunk(c, carry):
        r0 = pl.multiple_of(c * CHUNK, CHUNK)
        rows = pl.ds(r0, CHUNK)
        win = pl.ds(r0, BAND_PAST + CHUNK)
        first_valid = jnp.where(j == 0, BAND_PAST - c * CHUNK, 0)
        o = _attend_heads(q_ref[0, rows, :], kwin_ref[win, :], vwin_ref[win, :], bias_ref, first_valid)
        o_ref[0, rows, :] = o * gb_ref[0, rows, :]
        return carry

    lax.fori_loop(0, tq // CHUNK, chunk, 0)


def _attn_prompt(q, k, v, gb, bias):
    b, t, _ = q.shape
    tq = BAND_PAST
    assert t % tq == 0
    cur = pl.BlockSpec((1, tq, W_GRP), lambda i, j: (i, j, 0))
    prev = pl.BlockSpec((1, tq, W_GRP), lambda i, j: (i, jnp.maximum(j - 1, 0), 0))
    span = BAND_PAST + CHUNK
    return pl.pallas_call(
        _attn_prompt_kernel,
        grid=(b, t // tq),
        in_specs=[cur, prev, cur, prev, cur, cur,
                  pl.BlockSpec((N_HEADS, CHUNK, span), lambda i, j: (0, 0, 0))],
        out_specs=cur,
        out_shape=jax.ShapeDtypeStruct((b, t, W_GRP), F32),
        scratch_shapes=[pltpu.VMEM((2 * tq, W_GRP), BF16), pltpu.VMEM((2 * tq, W_GRP), BF16)],
        compiler_params=pltpu.CompilerParams(dimension_semantics=("arbitrary", "arbitrary"),
                                             vmem_limit_bytes=V7X_VMEM_LIMIT),
        name="attn_prompt",
    )(q, k, k, v, v, gb, bias)


def _attn_sample_kernel(q_ref, kn_ref, vn_ref, ck_ref, cv_ref, gb_ref, bias_ref, o_ref):
    kw = jnp.concatenate([ck_ref[0].astype(BF16), kn_ref[0]], axis=0)
    vw = jnp.concatenate([cv_ref[0].astype(BF16), vn_ref[0]], axis=0)
    o = _attend_heads(q_ref[0], kw, vw, bias_ref, 0)
    o_ref[0] = o * gb_ref[0]


def _attn_sample(q, k, v, ck, cv, gb, bias):
    b, tn, _ = q.shape
    wc = ck.shape[1]
    new = pl.BlockSpec((1, tn, W_GRP), lambda i: (i, 0, 0))
    old = pl.BlockSpec((1, wc, W_GRP), lambda i: (i, 0, 0))
    return pl.pallas_call(
        _attn_sample_kernel,
        grid=(b,),
        in_specs=[new, new, new, old, old, new, pl.BlockSpec((N_HEADS, tn, wc + tn), lambda i: (0, 0, 0))],
        out_specs=new,
        out_shape=jax.ShapeDtypeStruct((b, tn, W_GRP), F32),
        compiler_params=pltpu.CompilerParams(dimension_semantics=("arbitrary",),
                                             vmem_limit_bytes=V7X_VMEM_LIMIT),
        name="attn_sample",
    )(q, k, v, ck, cv, gb, bias)


def _outproj_kernel(x_ref, oa_ref, ob_ref, p_ref, wo_ref, wp_ref, pg_ref, wg_ref, y_ref):
    h = x_ref[0] + (_dot(oa_ref[0], wo_ref[0:W_GRP, :]) + _dot(ob_ref[0], wo_ref[W_GRP:2 * W_GRP, :]))
    e = _dot(p_ref[0], wp_ref[...])
    e = e * lax.rsqrt(jnp.mean(e * e, axis=-1, keepdims=True) + EPS) * pg_ref[...]
    y_ref[0] = h + _sigmoid(_dot(h, wg_ref[...])) * e


def _outproj(x, oa, ob, p, wo, wp, pg, wg, tm):
    b, t, d = x.shape
    pd = p.shape[-1]
    assert t % tm == 0
    row = lambda w: pl.BlockSpec((1, tm, w), lambda i, j: (i, j, 0))
    const = lambda s: pl.BlockSpec(s, lambda i, j: (0,) * len(s))
    return pl.pallas_call(
        _outproj_kernel,
        grid=(b, t // tm),
        in_specs=[row(d), row(W_GRP), row(W_GRP), row(pd), const((2 * W_GRP, d)), const((pd, d)),
                  const((1, d)), const((d, d))],
        out_specs=row(d),
        out_shape=jax.ShapeDtypeStruct((b, t, d), F32),
        compiler_params=pltpu.CompilerParams(dimension_semantics=("arbitrary", "arbitrary"),
                                             vmem_limit_bytes=V7X_VMEM_LIMIT),
        name="outproj",
    )(x, oa, ob, p, wo, wp, pg, wg)


def _rel_bias_table(table, q0, nq, nk):
    m = (q0 + nq - 1) - np.arange(nk + nq - 1)
    rev = table[:, np.clip(m, -MAX_REL, MAX_REL) + MAX_REL].astype(F32)
    return jnp.stack([rev[:, nq - 1 - i:nq - 1 - i + nk] for i in range(nq)], axis=1)


def _pad_lanes(v, width):
    return jnp.pad(v.reshape(1, -1), ((0, 0), (0, width - v.shape[-1])))


def _layer_weights(ln_g, w_in, conv_w, a_log, dt_bias, gdn_norm_g, q_norm_g, k_norm_g, w_out, w_ple_proj,
                   ple_norm_g, w_ple_gate):
    ab0 = QKV_A
    ab1 = QKV_A + 2 * N_HEADS
    w_r = jnp.concatenate([w_in[:, :ab0], w_in[:, ab1:],
                           jnp.pad(w_in[:, ab0:ab1], ((0, 0), (0, AB_PAD - 2 * N_HEADS)))], axis=1)
    head = jnp.arange(W_GRP) // D_HEAD
    return dict(
        ln_g=ln_g.reshape(1, -1), w_r=w_r.astype(BF16),
        qg=jnp.tile(q_norm_g, N_HEADS).reshape(1, -1), kg=jnp.tile(k_norm_g, N_HEADS).reshape(1, -1),
        bd=(head[:, None] == head[None, :]).astype(BF16),
        cw=conv_w, alog=_pad_lanes(a_log, AB_PAD), dtb=_pad_lanes(dt_bias, AB_PAD),
        ng=jnp.tile(gdn_norm_g, N_HEADS).reshape(1, -1),
        wo=w_out.astype(BF16), wp=w_ple_proj.astype(BF16), pg=ple_norm_g.reshape(1, -1),
        wg=w_ple_gate.astype(BF16))


def _state_to_lanes(s):
    b = s.shape[0]
    return s.transpose(0, 2, 1, 3).reshape(b, D_HEAD, W_GRP)


def _state_from_lanes(s):
    b = s.shape[0]
    return s.reshape(b, D_HEAD, N_HEADS, D_HEAD).transpose(0, 2, 1, 3)


def _layer(h, p_i, conv0, s0, caches, bias, w, tm, tt_in):
    qkv, ab, ga, q, k, v, gb, kst, vst = _inproj(h, w["ln_g"], w["w_r"], w["qg"], w["kg"], w["bd"], tm)
    if caches is not None:
        nb = caches[0].shape[0]
        regroup = lambda a: a.reshape(nb, a.shape[1] // nb, a.shape[-1])
        qkv, ab, ga, q, k, v, gb = map(regroup, (qkv, ab, ga, q, k, v, gb))
    oa, conv_new, s_new = _gdn(qkv, ab, ga, conv0, _state_to_lanes(s0), w["cw"], w["alog"], w["dtb"], w["ng"],
                               w["bd"], tt_in)
    if caches is None:
        ob = _attn_prompt(q, k, v, gb, bias)
    else:
        ob = _attn_sample(q, k, v, caches[0], caches[1], gb, bias)
        oa, ob = oa.reshape(h.shape[0], h.shape[1], W_GRP), ob.reshape(h.shape[0], h.shape[1], W_GRP)
    y = _outproj(h, oa, ob, p_i, w["wo"], w["wp"], w["pg"], w["wg"], tm)
    return y, conv_new[:, CONV_ROWS - (CONV_W - 1):], _state_from_lanes(s_new), kst, vst


def kernel(x_prompt, x_sample, state_conv, state_gdn, cache_k, cache_v, p_prompt, p_sample, ln_g, w_in, conv_w, a_log, dt_bias, gdn_norm_g, q_norm_g, k_norm_g, rel_bias, w_out, w_ple_proj, ple_norm_g, w_ple_gate):
    depth = ln_g.shape[0]
    bp, tp, d = x_prompt.shape
    bs, ts, _ = x_sample.shape
    wc = cache_k.shape[2]
    assert tp % BAND_PAST == 0 and ts <= CHUNK and ts % CONV_ROWS == 0
    hp = x_prompt
    hs = x_sample.reshape(1, bs * ts, d)
    outs = [[] for _ in range(8)]
    for i in range(depth):
        w = _layer_weights(ln_g[i], w_in[i], conv_w[i], a_log[i], dt_bias[i], gdn_norm_g[i], q_norm_g[i],
                           k_norm_g[i], w_out[i], w_ple_proj[i], ple_norm_g[i], w_ple_gate[i])
        conv0 = jnp.zeros((bp, CONV_ROWS, QKV_A), F32)
        s0 = jnp.zeros((bp, N_HEADS, D_HEAD, D_HEAD), F32)
        bias_p = _rel_bias_table(rel_bias[i], BAND_PAST, CHUNK, BAND_PAST + CHUNK)
        hp, c_p, g_p, k_p, v_p = _layer(hp, p_prompt[i], conv0, s0, None, bias_p, w,
                                        tm=BAND_PAST, tt_in=4 * CHUNK)
        conv0_s = jnp.pad(state_conv[i], ((0, 0), (CONV_ROWS - (CONV_W - 1), 0), (0, 0)))
        caches = (cache_k[i].reshape(bs, wc, W_GRP), cache_v[i].reshape(bs, wc, W_GRP))
        bias_s = _rel_bias_table(rel_bias[i], wc, ts, wc + ts)
        hs, c_s, g_s, k_s, v_s = _layer(hs, p_sample[i].reshape(1, bs * ts, -1), conv0_s, state_gdn[i],
                                        caches, bias_s, w, tm=bs * ts, tt_in=ts)
        new = (c_p, g_p, k_p.reshape(bp, BAND_PAST, N_HEADS, D_HEAD), v_p.reshape(bp, BAND_PAST, N_HEADS, D_HEAD),
               c_s, g_s, k_s.reshape(bs, ts, N_HEADS, D_HEAD), v_s.reshape(bs, ts, N_HEADS, D_HEAD))
        for lst, a in zip(outs, new):
            lst.append(a)
    return (hp, hs.reshape(bs, ts, d)) + tuple(jnp.stack(lst) for lst in outs)
```

```python
import functools

import numpy as np

import jax
import jax.numpy as jnp
from jax import lax
from jax.experimental import pallas as pl
from jax.experimental.pallas import tpu as pltpu

F32 = jnp.float32
BF16 = jnp.bfloat16
EPS = 1e-6

CHUNK = 64
N_HEADS = 8
D_HEAD = 64
W_GRP = N_HEADS * D_HEAD
QKV_A = 3 * W_GRP
CONV_W = 4
BAND_PAST = 8 * CHUNK
MAX_REL = 128
AB_PAD = 128
CONV_ROWS = 8
V7X_VMEM_LIMIT = 52 * 1024 * 1024
MXU_W = 256
MXU_HEADS = MXU_W // D_HEAD
N_DOUBLINGS = 5
Q_CHUNKS = 2
MASKED = -1e30

_OFF_ZA = QKV_A
_OFF_QB = _OFF_ZA + W_GRP
_OFF_KB = _OFF_QB + W_GRP
_OFF_VB = _OFF_KB + W_GRP
_OFF_ZB = _OFF_VB + W_GRP
_OFF_AB = _OFF_ZB + W_GRP
_D_PROJ_R = _OFF_AB + AB_PAD


def _dot(a, b):
    return jnp.dot(a.astype(BF16), b.astype(BF16), preferred_element_type=F32)


def _dot_split(x, w_bf16, n_terms):
    terms, r = [], x
    for _ in range(n_terms):
        t = r.astype(BF16)
        terms.append(t)
        r = r - t.astype(F32)
    acc = None
    for t in reversed(terms):
        d = jnp.dot(t, w_bf16, preferred_element_type=F32)
        acc = d if acc is None else acc + d
    return acc


def _sigmoid(x):
    return 1.0 / (1.0 + jnp.exp(-x))


def _silu(x):
    return x * _sigmoid(x)


def _softplus(x):
    return jnp.maximum(x, 0.0) + jnp.log1p(jnp.exp(-jnp.abs(x)))


def _head_mean_square(y, blockdiag_ones):
    return jnp.dot((y * y).astype(BF16), blockdiag_ones, preferred_element_type=F32) * (1.0 / D_HEAD)


def _inproj_kernel(x_ref, lng_ref, w_ref, qg_ref, kg_ref, bd_ref,
                   qkv_ref, ab_ref, ga_ref, q_ref, k_ref, v_ref, gb_ref, kst_ref, vst_ref):
    x = x_ref[0]
    xn = x * lax.rsqrt(jnp.mean(x * x, axis=-1, keepdims=True) + EPS) * lng_ref[...]
    xb = xn.astype(BF16)

    def proj(lo, hi):
        return jnp.dot(xb, w_ref[:, lo:hi], preferred_element_type=F32)

    bd = bd_ref[...]
    qkv_ref[0] = proj(0, QKV_A)
    ga_ref[0] = _silu(proj(_OFF_ZA, _OFF_QB))
    qb = proj(_OFF_QB, _OFF_KB)
    qn = qb * lax.rsqrt(_head_mean_square(qb, bd) + EPS) * qg_ref[...]
    q_ref[0] = (qn * (D_HEAD ** -0.5)).astype(BF16)
    kb = proj(_OFF_KB, _OFF_VB)
    kn = kb * lax.rsqrt(_head_mean_square(kb, bd) + EPS) * kg_ref[...]
    k_ref[0] = kn.astype(BF16)
    kst_ref[0] = kn
    vb = proj(_OFF_VB, _OFF_ZB)
    v_ref[0] = vb.astype(BF16)
    vst_ref[0] = vb
    gb_ref[0] = _silu(proj(_OFF_ZB, _OFF_AB))
    ab_ref[0] = proj(_OFF_AB, _D_PROJ_R)


def _inproj(x, ln_g, w_r, qg, kg, bd, tm):
    b, t, d = x.shape
    assert t % tm == 0
    grid = (b, t // tm)
    row = lambda w: pl.BlockSpec((1, tm, w), lambda i, j: (i, j, 0))
    const = lambda s: pl.BlockSpec(s, lambda i, j: (0,) * len(s))
    last = pl.BlockSpec((1, tm, W_GRP), lambda i, j: (i, 0, 0))
    sds = lambda w, dt: jax.ShapeDtypeStruct((b, t, w), dt)
    return pl.pallas_call(
        _inproj_kernel,
        grid=grid,
        in_specs=[row(d), const((1, d)), const((d, _D_PROJ_R)), const((1, W_GRP)), const((1, W_GRP)),
                  const((W_GRP, W_GRP))],
        out_specs=[row(QKV_A), row(AB_PAD), row(W_GRP), row(W_GRP), row(W_GRP), row(W_GRP), row(W_GRP),
                   last, last],
        out_shape=[sds(QKV_A, F32), sds(AB_PAD, F32), sds(W_GRP, F32), sds(W_GRP, BF16), sds(W_GRP, BF16),
                   sds(W_GRP, BF16), sds(W_GRP, F32),
                   jax.ShapeDtypeStruct((b, tm, W_GRP), F32), jax.ShapeDtypeStruct((b, tm, W_GRP), F32)],
        compiler_params=pltpu.CompilerParams(dimension_semantics=("arbitrary", "arbitrary"),
                                             vmem_limit_bytes=V7X_VMEM_LIMIT),
        name="inproj",
    )(x, ln_g, w_r, qg, kg, bd)


def _tile_rows(x, n):
    return jnp.concatenate([x] * n, axis=0)


def _gdn_tile(q_s, k_s, v_s, gcx_s, bex_s, o_s, s_ref, bdm_ref, bdmf_ref, lvl_ref, n_chunks):
    row = lax.broadcasted_iota(jnp.int32, (CHUNK, MXU_W), 0)
    col = jnp.bitwise_and(lax.broadcasted_iota(jnp.int32, (CHUNK, MXU_W), 1), D_HEAD - 1)
    incl = row >= col
    strict = row > col
    diag = row == col
    pair = strict & (jnp.right_shift(row, 1) == jnp.right_shift(col, 1))
    bdm = bdm_ref[...]
    blockdiag = lambda x: _tile_rows(x.astype(BF16), MXU_HEADS) * bdm
    mm = lambda a, b: jnp.dot(a, b, preferred_element_type=F32)

    chains = []
    for c in range(n_chunks):
        rows = slice(c * CHUNK, (c + 1) * CHUNK)
        for grp in range(N_HEADS // MXU_HEADS):
            ls = slice(grp * MXU_W, (grp + 1) * MXU_W)
            qc, kc, vc, gcx, bex = (r[rows, ls] for r in (q_s, k_s, v_s, gcx_s, bex_s))
            gc_row = jnp.sum(jnp.where(diag, gcx, 0.0), axis=0, keepdims=True)
            gc_last = gcx[CHUNK - 1:CHUNK, :]
            egc = jnp.exp(gcx)
            k16 = kc.astype(BF16)
            chains.append(dict(
                rows=rows, ls=ls, bex=bex, k16=k16,
                decay=jnp.where(incl, jnp.exp(jnp.where(incl, gcx - gc_row, 0.0)), 0.0),
                g_tot=jnp.exp(gc_last), u_rhs=vc * bex, w_rhs=kc * (bex * egc),
                qk_lhs=jnp.concatenate([qc.astype(BF16), k16], axis=0),
                q_dec=(qc * egc).astype(BF16), k_tail=(kc * jnp.exp(gc_last - gcx)).astype(BF16)))
    for ch in chains:
        ch["sc"] = lax.dot_general(ch["qk_lhs"], blockdiag(ch["k16"]), (((1,), (1,)), ((), ())),
                                   preferred_element_type=F32)
    for ch in chains:
        sc = ch.pop("sc")
        ch["qk"] = (sc[0:CHUNK] * ch["decay"]).astype(BF16)
        a_low = jnp.where(strict, ch["bex"] * sc[CHUNK:2 * CHUNK] * ch["decay"], 0.0)
        ch["a_rows"] = _tile_rows(a_low.astype(BF16), MXU_HEADS)
        ch["dinv"] = jnp.where(diag, 1.0, 0.0) - jnp.where(pair, a_low, 0.0)
    for li in range(N_DOUBLINGS):
        for ch in chains:
            ch["t1"] = mm(ch["dinv"].astype(BF16), ch["a_rows"] * lvl_ref[li])
        for ch in chains:
            ch["dinv"] = ch["dinv"] - mm(ch.pop("t1").astype(BF16), blockdiag(ch["dinv"]))
    for ch in chains:
        d16 = ch["dinv"].astype(BF16)
        ch["u"] = mm(d16, blockdiag(ch["u_rhs"]))
        ch["w"] = mm(d16, blockdiag(ch["w_rhs"]))
    for ch in chains:
        rows, ls = ch["rows"], ch["ls"]
        s = s_ref[:, ls]
        wq = mm(jnp.concatenate([ch["w"].astype(BF16), ch["q_dec"]], axis=0), blockdiag(s))
        v16 = (ch["u"] - wq[0:CHUNK]).astype(BF16)
        o_s[rows, ls] = wq[CHUNK:2 * CHUNK] + mm(ch["qk"], _tile_rows(v16, MXU_HEADS) * bdm)
        r = lax.dot_general(ch["k_tail"], v16, (((0,), (0,)), ((), ())),
                            preferred_element_type=F32) * bdmf_ref[...]
        s_ref[:, ls] = s * ch["g_tot"] + ((r[0:D_HEAD] + r[D_HEAD:2 * D_HEAD])
                                          + (r[2 * D_HEAD:3 * D_HEAD] + r[3 * D_HEAD:4 * D_HEAD]))


def _gdn_kernel(qkv_ref, ab_ref, ga_ref, conv0_ref, s0_ref, cw_ref, alog_ref, dtb_ref, ng_ref, bdones_ref,
                ea_ref, eb_ref, bdm_ref, bdmf_ref, lvl_ref,
                o_ref, convout_ref, sout_ref,
                xcat_ref, q_s, k_s, v_s, gcx_s, bex_s, o_s, s_ref, *, tt_in, tt):
    t = pl.program_id(1)

    @pl.when(t == 0)
    def _():
        xcat_ref[0:CONV_ROWS, :] = conv0_ref[0]
        s_ref[...] = s0_ref[0]

    xcat_ref[CONV_ROWS:CONV_ROWS + tt_in, :] = qkv_ref[0]
    if tt > tt_in:
        xcat_ref[CONV_ROWS + tt_in:CONV_ROWS + tt, :] = jnp.zeros((tt - tt_in, QKV_A), F32)
    cw = cw_ref[...]
    first = CONV_ROWS - (CONV_W - 1)
    conv = xcat_ref[first:first + tt, :] * cw[0:1, :]
    for j in range(1, CONV_W):
        conv = conv + xcat_ref[first + j:first + j + tt, :] * cw[j:j + 1, :]
    carry = xcat_ref[tt_in:tt_in + CONV_ROWS, :]
    convout_ref[0] = carry
    xcat_ref[0:CONV_ROWS, :] = carry

    c = _silu(conv)
    bdones = bdones_ref[...]
    q = c[:, 0:W_GRP]
    k = c[:, W_GRP:2 * W_GRP]
    q_s[...] = q * lax.rsqrt(_head_mean_square(q, bdones) * D_HEAD + EPS) * (D_HEAD ** -0.5)
    k_s[...] = k * lax.rsqrt(_head_mean_square(k, bdones) * D_HEAD + EPS)
    v_s[...] = c[:, 2 * W_GRP:3 * W_GRP]

    ab = ab_ref[0]
    g = -jnp.exp(alog_ref[...]) * _softplus(ab + dtb_ref[...])
    beta = _sigmoid(ab)
    if tt > tt_in:
        pad = jnp.zeros((tt - tt_in, AB_PAD), F32)
        g = jnp.concatenate([g, pad], axis=0)
        beta = jnp.concatenate([beta, pad], axis=0)
    rin = jnp.bitwise_and(lax.broadcasted_iota(jnp.int32, (tt, AB_PAD), 0), CHUNK - 1)
    gc = g
    step = 1
    while step < CHUNK:
        gc = gc + jnp.where(rin >= step, pltpu.roll(gc, step, 0), 0.0)
        step *= 2
    gcx_s[...] = _dot_split(gc, ea_ref[...], 3)
    bex_s[...] = _dot_split(beta, eb_ref[...], 2)

    _gdn_tile(q_s, k_s, v_s, gcx_s, bex_s, o_s, s_ref, bdm_ref, bdmf_ref, lvl_ref, tt // CHUNK)

    o = o_s[0:tt_in, :]
    o_ref[0] = o * lax.rsqrt(_head_mean_square(o, bdones) + EPS) * ng_ref[...] * ga_ref[0]

    @pl.when(t == pl.num_programs(1) - 1)
    def _():
        sout_ref[0] = s_ref[...]


def _gdn_masks():
    idx = np.arange(MXU_W)
    head, pos = idx // D_HEAD, idx % D_HEAD
    bdm = head[:, None] == head[None, :]
    levels = []
    for li in range(N_DOUBLINGS):
        n = 2 << li
        pr, pc = pos[:, None], pos[None, :]
        levels.append(bdm & (pr // (2 * n) == pc // (2 * n)) & (pr // n != pc // n))
    expand = np.arange(W_GRP)[None, :] // D_HEAD == np.arange(AB_PAD)[:, None]
    return dict(bdm=jnp.asarray(bdm, BF16), bdmf=jnp.asarray(bdm, F32),
                lvl=jnp.asarray(np.stack(levels), BF16),
                ea=jnp.asarray(expand, BF16), eb=jnp.asarray(np.roll(expand, N_HEADS, axis=0), BF16))


def _gdn(qkv, ab, ga, conv0, s0, cw, alog, dtb, ng, bdones, tt_in):
    b, t, _ = qkv.shape
    tt = -(-tt_in // CHUNK) * CHUNK
    assert t % tt_in == 0 and tt_in >= CONV_ROWS and (tt == tt_in or t == tt_in)
    m = _gdn_masks()
    row = lambda w: pl.BlockSpec((1, tt_in, w), lambda i, j: (i, j, 0))
    const = lambda s: pl.BlockSpec(s, lambda i, j: (0,) * len(s))
    per_b = lambda s: pl.BlockSpec((1,) + s, lambda i, j: (i,) + (0,) * len(s))
    return pl.pallas_call(
        functools.partial(_gdn_kernel, tt_in=tt_in, tt=tt),
        grid=(b, t // tt_in),
        in_specs=[row(QKV_A), row(AB_PAD), row(W_GRP), per_b((CONV_ROWS, QKV_A)), per_b((D_HEAD, W_GRP)),
                  const((CONV_W, QKV_A)), const((1, AB_PAD)), const((1, AB_PAD)), const((1, W_GRP)),
                  const((W_GRP, W_GRP)), const((AB_PAD, W_GRP)), const((AB_PAD, W_GRP)),
                  const((MXU_W, MXU_W)), const((MXU_W, MXU_W)), const((N_DOUBLINGS, MXU_W, MXU_W))],
        out_specs=[row(W_GRP), per_b((CONV_ROWS, QKV_A)), per_b((D_HEAD, W_GRP))],
        out_shape=[jax.ShapeDtypeStruct((b, t, W_GRP), F32),
                   jax.ShapeDtypeStruct((b, CONV_ROWS, QKV_A), F32),
                   jax.ShapeDtypeStruct((b, D_HEAD, W_GRP), F32)],
        scratch_shapes=[pltpu.VMEM((tt + CONV_ROWS, QKV_A), F32)]
        + [pltpu.VMEM((tt, W_GRP), F32)] * 6 + [pltpu.VMEM((D_HEAD, W_GRP), F32)],
        compiler_params=pltpu.CompilerParams(dimension_semantics=("arbitrary", "arbitrary"),
                                             vmem_limit_bytes=V7X_VMEM_LIMIT),
        name="gdn",
    )(qkv, ab, ga, conv0, s0, cw, alog, dtb, ng, bdones, m["ea"], m["eb"], m["bdm"], m["bdmf"], m["lvl"])


def _attend_heads(q, kw, vw, bias_ref, first_valid):
    lq, span = q.shape[0], kw.shape[0]
    col = lax.broadcasted_iota(jnp.int32, (lq, span), 1)
    valid = col >= first_valid
    head = lambda h: slice(h * D_HEAD, (h + 1) * D_HEAD)
    scores = lambda h: lax.dot_general(q[:, head(h)], kw[:, head(h)], (((1,), (1,)), ((), ())),
                                       preferred_element_type=F32)
    outs = []
    ahead = 3
    pending = [scores(h) for h in range(ahead)]
    for h in range(N_HEADS):
        s = pending.pop(0)
        if h + ahead < N_HEADS:
            pending.append(scores(h + ahead))
        s = jnp.where(valid, s + bias_ref[h], -jnp.inf)
        m = jnp.max(s, axis=-1, keepdims=True)
        p = jnp.exp(s - m)
        l = jnp.sum(p, axis=-1, keepdims=True)
        o = jnp.dot(p.astype(BF16), vw[:, head(h)], preferred_element_type=F32)
        outs.append(o / l)
    return jnp.concatenate(outs, axis=1)


def _attn_prompt_kernel(q_ref, kp_ref, kc_ref, vp_ref, vc_ref, gb_ref, bias_ref, o_ref, kwin_ref, vwin_ref):
    j = pl.program_id(1)
    tq = BAND_PAST
    kwin_ref[0:tq, :] = kp_ref[0]
    kwin_ref[tq:2 * tq, :] = kc_ref[0]
    vwin_ref[0:tq, :] = vp_ref[0]
    vwin_ref[tq:2 * tq, :] = vc_ref[0]

    def chunk_group(c, carry):
        r0 = pl.multiple_of(c * (Q_CHUNKS * CHUNK), Q_CHUNKS * CHUNK)
        rows = pl.ds(r0, Q_CHUNKS * CHUNK)
        win = pl.ds(r0, BAND_PAST + Q_CHUNKS * CHUNK)
        first_valid = jnp.where(j == 0, BAND_PAST - r0, 0)
        o = _attend_heads(q_ref[0, rows, :], kwin_ref[win, :], vwin_ref[win, :], bias_ref, first_valid)
        o_ref[0, rows, :] = o * gb_ref[0, rows, :]
        return carry

    lax.fori_loop(0, tq // (Q_CHUNKS * CHUNK), chunk_group, 0)


def _attn_prompt(q, k, v, gb, bias):
    b, t, _ = q.shape
    tq = BAND_PAST
    assert t % tq == 0
    cur = pl.BlockSpec((1, tq, W_GRP), lambda i, j: (i, j, 0))
    prev = pl.BlockSpec((1, tq, W_GRP), lambda i, j: (i, jnp.maximum(j - 1, 0), 0))
    return pl.pallas_call(
        _attn_prompt_kernel,
        grid=(b, t // tq),
        in_specs=[cur, prev, cur, prev, cur, cur,
                  pl.BlockSpec(bias.shape, lambda i, j: (0, 0, 0))],
        out_specs=cur,
        out_shape=jax.ShapeDtypeStruct((b, t, W_GRP), F32),
        scratch_shapes=[pltpu.VMEM((2 * tq, W_GRP), BF16), pltpu.VMEM((2 * tq, W_GRP), BF16)],
        compiler_params=pltpu.CompilerParams(dimension_semantics=("arbitrary", "arbitrary"),
                                             vmem_limit_bytes=V7X_VMEM_LIMIT),
        name="attn_prompt",
    )(q, k, k, v, v, gb, bias)


def _attn_sample_kernel(q_ref, kn_ref, vn_ref, ck_ref, cv_ref, gb_ref, bias_ref, o_ref):
    kw = jnp.concatenate([ck_ref[0].astype(BF16), kn_ref[0]], axis=0)
    vw = jnp.concatenate([cv_ref[0].astype(BF16), vn_ref[0]], axis=0)
    o = _attend_heads(q_ref[0], kw, vw, bias_ref, 0)
    o_ref[0] = o * gb_ref[0]


def _attn_sample(q, k, v, ck, cv, gb, bias):
    b, tn, _ = q.shape
    wc = ck.shape[1]
    new = pl.BlockSpec((1, tn, W_GRP), lambda i: (i, 0, 0))
    old = pl.BlockSpec((1, wc, W_GRP), lambda i: (i, 0, 0))
    return pl.pallas_call(
        _attn_sample_kernel,
        grid=(b,),
        in_specs=[new, new, new, old, old, new, pl.BlockSpec((N_HEADS, tn, wc + tn), lambda i: (0, 0, 0))],
        out_specs=new,
        out_shape=jax.ShapeDtypeStruct((b, tn, W_GRP), F32),
        compiler_params=pltpu.CompilerParams(dimension_semantics=("arbitrary",),
                                             vmem_limit_bytes=V7X_VMEM_LIMIT),
        name="attn_sample",
    )(q, k, v, ck, cv, gb, bias)


def _outproj_kernel(x_ref, oa_ref, ob_ref, p_ref, wo_ref, wp_ref, pg_ref, wg_ref, y_ref):
    h = x_ref[0] + (_dot(oa_ref[0], wo_ref[0:W_GRP, :]) + _dot(ob_ref[0], wo_ref[W_GRP:2 * W_GRP, :]))
    e = _dot(p_ref[0], wp_ref[...])
    e = e * lax.rsqrt(jnp.mean(e * e, axis=-1, keepdims=True) + EPS) * pg_ref[...]
    y_ref[0] = h + _sigmoid(_dot(h, wg_ref[...])) * e


def _outproj(x, oa, ob, p, wo, wp, pg, wg, tm):
    b, t, d = x.shape
    pd = p.shape[-1]
    assert t % tm == 0
    row = lambda w: pl.BlockSpec((1, tm, w), lambda i, j: (i, j, 0))
    const = lambda s: pl.BlockSpec(s, lambda i, j: (0,) * len(s))
    return pl.pallas_call(
        _outproj_kernel,
        grid=(b, t // tm),
        in_specs=[row(d), row(W_GRP), row(W_GRP), row(pd), const((2 * W_GRP, d)), const((pd, d)),
                  const((1, d)), const((d, d))],
        out_specs=row(d),
        out_shape=jax.ShapeDtypeStruct((b, t, d), F32),
        compiler_params=pltpu.CompilerParams(dimension_semantics=("arbitrary", "arbitrary"),
                                             vmem_limit_bytes=V7X_VMEM_LIMIT),
        name="outproj",
    )(x, oa, ob, p, wo, wp, pg, wg)


def _rel_bias_table(table, q0, nq, nk):
    m = (q0 + nq - 1) - np.arange(nk + nq - 1)
    rev = table[:, np.clip(m, -MAX_REL, MAX_REL) + MAX_REL].astype(F32)
    return jnp.stack([rev[:, nq - 1 - i:nq - 1 - i + nk] for i in range(nq)], axis=1)


def _group_bias(bias):
    h = bias.shape[0]
    rows = []
    for c in range(Q_CHUNKS):
        left = jnp.full((h, CHUNK, c * CHUNK), MASKED, F32)
        right = jnp.full((h, CHUNK, (Q_CHUNKS - 1 - c) * CHUNK), MASKED, F32)
        rows.append(jnp.concatenate([left, bias, right], axis=2))
    return jnp.concatenate(rows, axis=1)


def _pad_lanes(v, width):
    return jnp.pad(v.reshape(1, -1), ((0, 0), (0, width - v.shape[-1])))


def _layer_weights(ln_g, w_in, conv_w, a_log, dt_bias, gdn_norm_g, q_norm_g, k_norm_g, w_out, w_ple_proj,
                   ple_norm_g, w_ple_gate):
    ab0 = QKV_A
    ab1 = QKV_A + 2 * N_HEADS
    w_r = jnp.concatenate([w_in[:, :ab0], w_in[:, ab1:],
                           jnp.pad(w_in[:, ab0:ab1], ((0, 0), (0, AB_PAD - 2 * N_HEADS)))], axis=1)
    head = jnp.arange(W_GRP) // D_HEAD
    return dict(
        ln_g=ln_g.reshape(1, -1), w_r=w_r.astype(BF16),
        qg=jnp.tile(q_norm_g, N_HEADS).reshape(1, -1), kg=jnp.tile(k_norm_g, N_HEADS).reshape(1, -1),
        bd=(head[:, None] == head[None, :]).astype(BF16),
        cw=conv_w, alog=_pad_lanes(a_log, AB_PAD), dtb=_pad_lanes(dt_bias, AB_PAD),
        ng=jnp.tile(gdn_norm_g, N_HEADS).reshape(1, -1),
        wo=w_out.astype(BF16), wp=w_ple_proj.astype(BF16), pg=ple_norm_g.reshape(1, -1),
        wg=w_ple_gate.astype(BF16))


def _state_to_lanes(s):
    b = s.shape[0]
    return s.transpose(0, 2, 1, 3).reshape(b, D_HEAD, W_GRP)


def _state_from_lanes(s):
    b = s.shape[0]
    return s.reshape(b, D_HEAD, N_HEADS, D_HEAD).transpose(0, 2, 1, 3)


def _layer(h, p_i, conv0, s0, caches, bias, w, tm, tt_in):
    qkv, ab, ga, q, k, v, gb, kst, vst = _inproj(h, w["ln_g"], w["w_r"], w["qg"], w["kg"], w["bd"], tm)
    if caches is not None:
        nb = caches[0].shape[0]
        regroup = lambda a: a.reshape(nb, a.shape[1] // nb, a.shape[-1])
        qkv, ab, ga, q, k, v, gb = map(regroup, (qkv, ab, ga, q, k, v, gb))
    oa, conv_new, s_new = _gdn(qkv, ab, ga, conv0, _state_to_lanes(s0), w["cw"], w["alog"], w["dtb"], w["ng"],
                               w["bd"], tt_in)
    if caches is None:
        ob = _attn_prompt(q, k, v, gb, bias)
    else:
        ob = _attn_sample(q, k, v, caches[0], caches[1], gb, bias)
        oa, ob = oa.reshape(h.shape[0], h.shape[1], W_GRP), ob.reshape(h.shape[0], h.shape[1], W_GRP)
    y = _outproj(h, oa, ob, p_i, w["wo"], w["wp"], w["pg"], w["wg"], tm)
    return y, conv_new[:, CONV_ROWS - (CONV_W - 1):], _state_from_lanes(s_new), kst, vst


def kernel(x_prompt, x_sample, state_conv, state_gdn, cache_k, cache_v, p_prompt, p_sample, ln_g, w_in, conv_w, a_log, dt_bias, gdn_norm_g, q_norm_g, k_norm_g, rel_bias, w_out, w_ple_proj, ple_norm_g, w_ple_gate):
    depth = ln_g.shape[0]
    bp, tp, d = x_prompt.shape
    bs, ts, _ = x_sample.shape
    wc = cache_k.shape[2]
    assert tp % BAND_PAST == 0 and ts <= CHUNK and ts % CONV_ROWS == 0
    hp = x_prompt
    hs = x_sample.reshape(1, bs * ts, d)
    outs = [[] for _ in range(8)]
    for i in range(depth):
        w = _layer_weights(ln_g[i], w_in[i], conv_w[i], a_log[i], dt_bias[i], gdn_norm_g[i], q_norm_g[i],
                           k_norm_g[i], w_out[i], w_ple_proj[i], ple_norm_g[i], w_ple_gate[i])
        conv0 = jnp.zeros((bp, CONV_ROWS, QKV_A), F32)
        s0 = jnp.zeros((bp, N_HEADS, D_HEAD, D_HEAD), F32)
        bias_p = _group_bias(_rel_bias_table(rel_bias[i], BAND_PAST, CHUNK, BAND_PAST + CHUNK))
        hp, c_p, g_p, k_p, v_p = _layer(hp, p_prompt[i], conv0, s0, None, bias_p, w,
                                        tm=BAND_PAST, tt_in=4 * CHUNK)
        conv0_s = jnp.pad(state_conv[i], ((0, 0), (CONV_ROWS - (CONV_W - 1), 0), (0, 0)))
        caches = (cache_k[i].reshape(bs, wc, W_GRP), cache_v[i].reshape(bs, wc, W_GRP))
        bias_s = _rel_bias_table(rel_bias[i], wc, ts, wc + ts)
        hs, c_s, g_s, k_s, v_s = _layer(hs, p_sample[i].reshape(1, bs * ts, -1), conv0_s, state_gdn[i],
                                        caches, bias_s, w, tm=bs * ts, tt_in=ts)
        new = (c_p, g_p, k_p.reshape(bp, BAND_PAST, N_HEADS, D_HEAD), v_p.reshape(bp, BAND_PAST, N_HEADS, D_HEAD),
               c_s, g_s, k_s.reshape(bs, ts, N_HEADS, D_HEAD), v_s.reshape(bs, ts, N_HEADS, D_HEAD))
        for lst, a in zip(outs, new):
            lst.append(a)
    return (hp, hs.reshape(bs, ts, d)) + tuple(jnp.stack(lst) for lst in outs)
```

```python
import functools

import numpy as np

import jax
import jax.numpy as jnp
from jax import lax
from jax.experimental import pallas as pl
from jax.experimental.pallas import tpu as pltpu

F32 = jnp.float32
BF16 = jnp.bfloat16
EPS = 1e-6

CHUNK = 64
N_HEADS = 8
D_HEAD = 64
W_GRP = N_HEADS * D_HEAD
QKV_A = 3 * W_GRP
CONV_W = 4
BAND_PAST = 8 * CHUNK
MAX_REL = 128
AB_PAD = 128
CONV_ROWS = 8
V7X_VMEM_LIMIT = 52 * 1024 * 1024
MXU_W = 256
MXU_HEADS = MXU_W // D_HEAD
N_DOUBLINGS = 5
Q_CHUNKS = 4
MASKED = -1e30

_OFF_ZA = QKV_A
_OFF_QB = _OFF_ZA + W_GRP
_OFF_KB = _OFF_QB + W_GRP
_OFF_VB = _OFF_KB + W_GRP
_OFF_ZB = _OFF_VB + W_GRP
_OFF_AB = _OFF_ZB + W_GRP
_D_PROJ_R = _OFF_AB + AB_PAD


def _dot(a, b):
    return jnp.dot(a.astype(BF16), b.astype(BF16), preferred_element_type=F32)


def _dot_split(x, w_bf16, n_terms):
    terms, r = [], x
    for _ in range(n_terms):
        t = r.astype(BF16)
        terms.append(t)
        r = r - t.astype(F32)
    acc = None
    for t in reversed(terms):
        d = jnp.dot(t, w_bf16, preferred_element_type=F32)
        acc = d if acc is None else acc + d
    return acc


def _sigmoid(x):
    return 1.0 / (1.0 + jnp.exp(-x))


def _silu(x):
    return x * _sigmoid(x)


def _softplus(x):
    return jnp.maximum(x, 0.0) + jnp.log1p(jnp.exp(-jnp.abs(x)))


def _head_mean_square(y, blockdiag_ones):
    return jnp.dot((y * y).astype(BF16), blockdiag_ones, preferred_element_type=F32) * (1.0 / D_HEAD)


def _inproj_kernel(x_ref, lng_ref, w_ref, qg_ref, kg_ref, bd_ref,
                   qkv_ref, ab_ref, ga_ref, q_ref, k_ref, v_ref, gb_ref, kst_ref, vst_ref):
    x = x_ref[0]
    xn = x * lax.rsqrt(jnp.mean(x * x, axis=-1, keepdims=True) + EPS) * lng_ref[...]
    xb = xn.astype(BF16)

    def proj(lo, hi):
        return jnp.dot(xb, w_ref[:, lo:hi], preferred_element_type=F32)

    bd = bd_ref[...]
    qkv_ref[0] = proj(0, QKV_A)
    ga_ref[0] = _silu(proj(_OFF_ZA, _OFF_QB))
    qb = proj(_OFF_QB, _OFF_KB)
    qn = qb * lax.rsqrt(_head_mean_square(qb, bd) + EPS) * qg_ref[...]
    q_ref[0] = (qn * (D_HEAD ** -0.5)).astype(BF16)
    kb = proj(_OFF_KB, _OFF_VB)
    kn = kb * lax.rsqrt(_head_mean_square(kb, bd) + EPS) * kg_ref[...]
    k_ref[0] = kn.astype(BF16)
    kst_ref[0] = kn
    vb = proj(_OFF_VB, _OFF_ZB)
    v_ref[0] = vb.astype(BF16)
    vst_ref[0] = vb
    gb_ref[0] = _silu(proj(_OFF_ZB, _OFF_AB))
    ab_ref[0] = proj(_OFF_AB, _D_PROJ_R)


def _inproj(x, ln_g, w_r, qg, kg, bd, tm):
    b, t, d = x.shape
    assert t % tm == 0
    grid = (b, t // tm)
    row = lambda w: pl.BlockSpec((1, tm, w), lambda i, j: (i, j, 0))
    const = lambda s: pl.BlockSpec(s, lambda i, j: (0,) * len(s))
    last = pl.BlockSpec((1, tm, W_GRP), lambda i, j: (i, 0, 0))
    sds = lambda w, dt: jax.ShapeDtypeStruct((b, t, w), dt)
    return pl.pallas_call(
        _inproj_kernel,
        grid=grid,
        in_specs=[row(d), const((1, d)), const((d, _D_PROJ_R)), const((1, W_GRP)), const((1, W_GRP)),
                  const((W_GRP, W_GRP))],
        out_specs=[row(QKV_A), row(AB_PAD), row(W_GRP), row(W_GRP), row(W_GRP), row(W_GRP), row(W_GRP),
                   last, last],
        out_shape=[sds(QKV_A, F32), sds(AB_PAD, F32), sds(W_GRP, F32), sds(W_GRP, BF16), sds(W_GRP, BF16),
                   sds(W_GRP, BF16), sds(W_GRP, F32),
                   jax.ShapeDtypeStruct((b, tm, W_GRP), F32), jax.ShapeDtypeStruct((b, tm, W_GRP), F32)],
        compiler_params=pltpu.CompilerParams(dimension_semantics=("arbitrary", "arbitrary"),
                                             vmem_limit_bytes=V7X_VMEM_LIMIT),
        name="inproj",
    )(x, ln_g, w_r, qg, kg, bd)


def _tile_rows(x, n):
    return jnp.concatenate([x] * n, axis=0)


def _gdn_tile(q_s, k_s, v_s, gcx_s, bex_s, o_s, s_ref, bdm_ref, bdmf_ref, lvl_ref, n_chunks):
    row = lax.broadcasted_iota(jnp.int32, (CHUNK, MXU_W), 0)
    col = jnp.bitwise_and(lax.broadcasted_iota(jnp.int32, (CHUNK, MXU_W), 1), D_HEAD - 1)
    incl = row >= col
    strict = row > col
    diag = row == col
    pair = strict & (jnp.right_shift(row, 1) == jnp.right_shift(col, 1))
    bdm = bdm_ref[...]
    blockdiag = lambda x: _tile_rows(x.astype(BF16), MXU_HEADS) * bdm
    mm = lambda a, b: jnp.dot(a, b, preferred_element_type=F32)

    chains = []
    for c in range(n_chunks):
        rows = slice(c * CHUNK, (c + 1) * CHUNK)
        for grp in range(N_HEADS // MXU_HEADS):
            ls = slice(grp * MXU_W, (grp + 1) * MXU_W)
            qc, kc, vc, gcx, bex = (r[rows, ls] for r in (q_s, k_s, v_s, gcx_s, bex_s))
            gc_row = jnp.sum(jnp.where(diag, gcx, 0.0), axis=0, keepdims=True)
            gc_last = gcx[CHUNK - 1:CHUNK, :]
            egc = jnp.exp(gcx)
            k16 = kc.astype(BF16)
            chains.append(dict(
                rows=rows, ls=ls, bex=bex, k16=k16,
                decay=jnp.where(incl, jnp.exp(jnp.where(incl, gcx - gc_row, 0.0)), 0.0),
                g_tot=jnp.exp(gc_last), u_rhs=vc * bex, w_rhs=kc * (bex * egc),
                qk_lhs=jnp.concatenate([qc.astype(BF16), k16], axis=0),
                q_dec=(qc * egc).astype(BF16), k_tail=(kc * jnp.exp(gc_last - gcx)).astype(BF16)))
    for ch in chains:
        ch["sc"] = lax.dot_general(ch["qk_lhs"], blockdiag(ch["k16"]), (((1,), (1,)), ((), ())),
                                   preferred_element_type=F32)
    for ch in chains:
        sc = ch.pop("sc")
        ch["qk"] = (sc[0:CHUNK] * ch["decay"]).astype(BF16)
        a_low = jnp.where(strict, ch["bex"] * sc[CHUNK:2 * CHUNK] * ch["decay"], 0.0)
        ch["a_rows"] = _tile_rows(a_low.astype(BF16), MXU_HEADS)
        ch["dinv"] = jnp.where(diag, 1.0, 0.0) - jnp.where(pair, a_low, 0.0)
    for li in range(N_DOUBLINGS):
        for ch in chains:
            ch["t1"] = mm(ch["dinv"].astype(BF16), ch["a_rows"] * lvl_ref[li])
        for ch in chains:
            ch["dinv"] = ch["dinv"] - mm(ch.pop("t1").astype(BF16), blockdiag(ch["dinv"]))
    for ch in chains:
        d16 = ch["dinv"].astype(BF16)
        ch["u"] = mm(d16, blockdiag(ch["u_rhs"]))
        ch["w"] = mm(d16, blockdiag(ch["w_rhs"]))
    for ch in chains:
        rows, ls = ch["rows"], ch["ls"]
        s = s_ref[:, ls]
        wq = mm(jnp.concatenate([ch["w"].astype(BF16), ch["q_dec"]], axis=0), blockdiag(s))
        v16 = (ch["u"] - wq[0:CHUNK]).astype(BF16)
        o_s[rows, ls] = wq[CHUNK:2 * CHUNK] + mm(ch["qk"], _tile_rows(v16, MXU_HEADS) * bdm)
        r = lax.dot_general(ch["k_tail"], v16, (((0,), (0,)), ((), ())),
                            preferred_element_type=F32) * bdmf_ref[...]
        s_ref[:, ls] = s * ch["g_tot"] + ((r[0:D_HEAD] + r[D_HEAD:2 * D_HEAD])
                                          + (r[2 * D_HEAD:3 * D_HEAD] + r[3 * D_HEAD:4 * D_HEAD]))


def _gdn_kernel(qkv_ref, ab_ref, ga_ref, conv0_ref, s0_ref, cw_ref, alog_ref, dtb_ref, ng_ref, bdones_ref,
                ea_ref, eb_ref, bdm_ref, bdmf_ref, lvl_ref,
                o_ref, convout_ref, sout_ref,
                xcat_ref, q_s, k_s, v_s, gcx_s, bex_s, o_s, s_ref, *, tt_in, tt):
    t = pl.program_id(1)

    @pl.when(t == 0)
    def _():
        xcat_ref[0:CONV_ROWS, :] = conv0_ref[0]
        s_ref[...] = s0_ref[0]

    xcat_ref[CONV_ROWS:CONV_ROWS + tt_in, :] = qkv_ref[0]
    if tt > tt_in:
        xcat_ref[CONV_ROWS + tt_in:CONV_ROWS + tt, :] = jnp.zeros((tt - tt_in, QKV_A), F32)
    cw = cw_ref[...]
    first = CONV_ROWS - (CONV_W - 1)
    conv = xcat_ref[first:first + tt, :] * cw[0:1, :]
    for j in range(1, CONV_W):
        conv = conv + xcat_ref[first + j:first + j + tt, :] * cw[j:j + 1, :]
    carry = xcat_ref[tt_in:tt_in + CONV_ROWS, :]
    convout_ref[0] = carry
    xcat_ref[0:CONV_ROWS, :] = carry

    c = _silu(conv)
    bdones = bdones_ref[...]
    q = c[:, 0:W_GRP]
    k = c[:, W_GRP:2 * W_GRP]
    q_s[...] = q * lax.rsqrt(_head_mean_square(q, bdones) * D_HEAD + EPS) * (D_HEAD ** -0.5)
    k_s[...] = k * lax.rsqrt(_head_mean_square(k, bdones) * D_HEAD + EPS)
    v_s[...] = c[:, 2 * W_GRP:3 * W_GRP]

    ab = ab_ref[0]
    g = -jnp.exp(alog_ref[...]) * _softplus(ab + dtb_ref[...])
    beta = _sigmoid(ab)
    if tt > tt_in:
        pad = jnp.zeros((tt - tt_in, AB_PAD), F32)
        g = jnp.concatenate([g, pad], axis=0)
        beta = jnp.concatenate([beta, pad], axis=0)
    rin = jnp.bitwise_and(lax.broadcasted_iota(jnp.int32, (tt, AB_PAD), 0), CHUNK - 1)
    gc = g
    step = 1
    while step < CHUNK:
        gc = gc + jnp.where(rin >= step, pltpu.roll(gc, step, 0), 0.0)
        step *= 2
    gcx_s[...] = _dot_split(gc, ea_ref[...], 3)
    bex_s[...] = _dot_split(beta, eb_ref[...], 2)

    _gdn_tile(q_s, k_s, v_s, gcx_s, bex_s, o_s, s_ref, bdm_ref, bdmf_ref, lvl_ref, tt // CHUNK)

    o = o_s[0:tt_in, :]
    o_ref[0] = o * lax.rsqrt(_head_mean_square(o, bdones) + EPS) * ng_ref[...] * ga_ref[0]

    @pl.when(t == pl.num_programs(1) - 1)
    def _():
        sout_ref[0] = s_ref[...]


def _gdn_masks():
    idx = np.arange(MXU_W)
    head, pos = idx // D_HEAD, idx % D_HEAD
    bdm = head[:, None] == head[None, :]
    levels = []
    for li in range(N_DOUBLINGS):
        n = 2 << li
        pr, pc = pos[:, None], pos[None, :]
        levels.append(bdm & (pr // (2 * n) == pc // (2 * n)) & (pr // n != pc // n))
    expand = np.arange(W_GRP)[None, :] // D_HEAD == np.arange(AB_PAD)[:, None]
    return dict(bdm=jnp.asarray(bdm, BF16), bdmf=jnp.asarray(bdm, F32),
                lvl=jnp.asarray(np.stack(levels), BF16),
                ea=jnp.asarray(expand, BF16), eb=jnp.asarray(np.roll(expand, N_HEADS, axis=0), BF16))


def _gdn(qkv, ab, ga, conv0, s0, cw, alog, dtb, ng, bdones, tt_in):
    b, t, _ = qkv.shape
    tt = -(-tt_in // CHUNK) * CHUNK
    assert t % tt_in == 0 and tt_in >= CONV_ROWS and (tt == tt_in or t == tt_in)
    m = _gdn_masks()
    row = lambda w: pl.BlockSpec((1, tt_in, w), lambda i, j: (i, j, 0))
    const = lambda s: pl.BlockSpec(s, lambda i, j: (0,) * len(s))
    per_b = lambda s: pl.BlockSpec((1,) + s, lambda i, j: (i,) + (0,) * len(s))
    return pl.pallas_call(
        functools.partial(_gdn_kernel, tt_in=tt_in, tt=tt),
        grid=(b, t // tt_in),
        in_specs=[row(QKV_A), row(AB_PAD), row(W_GRP), per_b((CONV_ROWS, QKV_A)), per_b((D_HEAD, W_GRP)),
                  const((CONV_W, QKV_A)), const((1, AB_PAD)), const((1, AB_PAD)), const((1, W_GRP)),
                  const((W_GRP, W_GRP)), const((AB_PAD, W_GRP)), const((AB_PAD, W_GRP)),
                  const((MXU_W, MXU_W)), const((MXU_W, MXU_W)), const((N_DOUBLINGS, MXU_W, MXU_W))],
        out_specs=[row(W_GRP), per_b((CONV_ROWS, QKV_A)), per_b((D_HEAD, W_GRP))],
        out_shape=[jax.ShapeDtypeStruct((b, t, W_GRP), F32),
                   jax.ShapeDtypeStruct((b, CONV_ROWS, QKV_A), F32),
                   jax.ShapeDtypeStruct((b, D_HEAD, W_GRP), F32)],
        scratch_shapes=[pltpu.VMEM((tt + CONV_ROWS, QKV_A), F32)]
        + [pltpu.VMEM((tt, W_GRP), F32)] * 6 + [pltpu.VMEM((D_HEAD, W_GRP), F32)],
        compiler_params=pltpu.CompilerParams(dimension_semantics=("arbitrary", "arbitrary"),
                                             vmem_limit_bytes=V7X_VMEM_LIMIT),
        name="gdn",
    )(qkv, ab, ga, conv0, s0, cw, alog, dtb, ng, bdones, m["ea"], m["eb"], m["bdm"], m["bdmf"], m["lvl"])


def _attend_heads(q, kw, vw, bias_ref, first_valid):
    lq, span = q.shape[0], kw.shape[0]
    col = lax.broadcasted_iota(jnp.int32, (lq, span), 1)
    valid = col >= first_valid
    head = lambda h: slice(h * D_HEAD, (h + 1) * D_HEAD)
    scores = lambda h: lax.dot_general(q[:, head(h)], kw[:, head(h)], (((1,), (1,)), ((), ())),
                                       preferred_element_type=F32)
    outs = []
    ahead = 3
    pending = [scores(h) for h in range(ahead)]
    for h in range(N_HEADS):
        s = pending.pop(0)
        if h + ahead < N_HEADS:
            pending.append(scores(h + ahead))
        s = jnp.where(valid, s + bias_ref[h], -jnp.inf)
        m = jnp.max(s, axis=-1, keepdims=True)
        p = jnp.exp(s - m)
        l = jnp.sum(p, axis=-1, keepdims=True)
        o = jnp.dot(p.astype(BF16), vw[:, head(h)], preferred_element_type=F32)
        outs.append(o / l)
    return jnp.concatenate(outs, axis=1)


def _attn_prompt_kernel(q_ref, kp_ref, kc_ref, vp_ref, vc_ref, gb_ref, bias_ref, o_ref, kwin_ref, vwin_ref):
    j = pl.program_id(1)
    tq = BAND_PAST
    kwin_ref[0:tq, :] = kp_ref[0]
    kwin_ref[tq:2 * tq, :] = kc_ref[0]
    vwin_ref[0:tq, :] = vp_ref[0]
    vwin_ref[tq:2 * tq, :] = vc_ref[0]

    def chunk_group(c, carry):
        r0 = pl.multiple_of(c * (Q_CHUNKS * CHUNK), Q_CHUNKS * CHUNK)
        rows = pl.ds(r0, Q_CHUNKS * CHUNK)
        win = pl.ds(r0, BAND_PAST + Q_CHUNKS * CHUNK)
        first_valid = jnp.where(j == 0, BAND_PAST - r0, 0)
        o = _attend_heads(q_ref[0, rows, :], kwin_ref[win, :], vwin_ref[win, :], bias_ref, first_valid)
        o_ref[0, rows, :] = o * gb_ref[0, rows, :]
        return carry

    lax.fori_loop(0, tq // (Q_CHUNKS * CHUNK), chunk_group, 0)


def _attn_prompt(q, k, v, gb, bias):
    b, t, _ = q.shape
    tq = BAND_PAST
    assert t % tq == 0
    cur = pl.BlockSpec((1, tq, W_GRP), lambda i, j: (i, j, 0))
    prev = pl.BlockSpec((1, tq, W_GRP), lambda i, j: (i, jnp.maximum(j - 1, 0), 0))
    return pl.pallas_call(
        _attn_prompt_kernel,
        grid=(b, t // tq),
        in_specs=[cur, prev, cur, prev, cur, cur,
                  pl.BlockSpec(bias.shape, lambda i, j: (0, 0, 0))],
        out_specs=cur,
        out_shape=jax.ShapeDtypeStruct((b, t, W_GRP), F32),
        scratch_shapes=[pltpu.VMEM((2 * tq, W_GRP), BF16), pltpu.VMEM((2 * tq, W_GRP), BF16)],
        compiler_params=pltpu.CompilerParams(dimension_semantics=("arbitrary", "arbitrary"),
                                             vmem_limit_bytes=V7X_VMEM_LIMIT),
        name="attn_prompt",
    )(q, k, k, v, v, gb, bias)


def _attn_sample_kernel(q_ref, kn_ref, vn_ref, ck_ref, cv_ref, gb_ref, bias_ref, o_ref):
    kw = jnp.concatenate([ck_ref[0].astype(BF16), kn_ref[0]], axis=0)
    vw = jnp.concatenate([cv_ref[0].astype(BF16), vn_ref[0]], axis=0)
    o = _attend_heads(q_ref[0], kw, vw, bias_ref, 0)
    o_ref[0] = o * gb_ref[0]


def _attn_sample(q, k, v, ck, cv, gb, bias):
    b, tn, _ = q.shape
    wc = ck.shape[1]
    new = pl.BlockSpec((1, tn, W_GRP), lambda i: (i, 0, 0))
    old = pl.BlockSpec((1, wc, W_GRP), lambda i: (i, 0, 0))
    return pl.pallas_call(
        _attn_sample_kernel,
        grid=(b,),
        in_specs=[new, new, new, old, old, new, pl.BlockSpec((N_HEADS, tn, wc + tn), lambda i: (0, 0, 0))],
        out_specs=new,
        out_shape=jax.ShapeDtypeStruct((b, tn, W_GRP), F32),
        compiler_params=pltpu.CompilerParams(dimension_semantics=("arbitrary",),
                                             vmem_limit_bytes=V7X_VMEM_LIMIT),
        name="attn_sample",
    )(q, k, v, ck, cv, gb, bias)


def _outproj_kernel(x_ref, oa_ref, ob_ref, p_ref, wo_ref, wp_ref, pg_ref, wg_ref, y_ref):
    h = x_ref[0] + (_dot(oa_ref[0], wo_ref[0:W_GRP, :]) + _dot(ob_ref[0], wo_ref[W_GRP:2 * W_GRP, :]))
    e = _dot(p_ref[0], wp_ref[...])
    e = e * lax.rsqrt(jnp.mean(e * e, axis=-1, keepdims=True) + EPS) * pg_ref[...]
    y_ref[0] = h + _sigmoid(_dot(h, wg_ref[...])) * e


def _outproj(x, oa, ob, p, wo, wp, pg, wg, tm):
    b, t, d = x.shape
    pd = p.shape[-1]
    assert t % tm == 0
    row = lambda w: pl.BlockSpec((1, tm, w), lambda i, j: (i, j, 0))
    const = lambda s: pl.BlockSpec(s, lambda i, j: (0,) * len(s))
    return pl.pallas_call(
        _outproj_kernel,
        grid=(b, t // tm),
        in_specs=[row(d), row(W_GRP), row(W_GRP), row(pd), const((2 * W_GRP, d)), const((pd, d)),
                  const((1, d)), const((d, d))],
        out_specs=row(d),
        out_shape=jax.ShapeDtypeStruct((b, t, d), F32),
        compiler_params=pltpu.CompilerParams(dimension_semantics=("arbitrary", "arbitrary"),
                                             vmem_limit_bytes=V7X_VMEM_LIMIT),
        name="outproj",
    )(x, oa, ob, p, wo, wp, pg, wg)


def _rel_bias_table(table, q0, nq, nk):
    m = (q0 + nq - 1) - np.arange(nk + nq - 1)
    rev = table[:, np.clip(m, -MAX_REL, MAX_REL) + MAX_REL].astype(F32)
    return jnp.stack([rev[:, nq - 1 - i:nq - 1 - i + nk] for i in range(nq)], axis=1)


def _group_bias(bias):
    h = bias.shape[0]
    rows = []
    for c in range(Q_CHUNKS):
        left = jnp.full((h, CHUNK, c * CHUNK), MASKED, F32)
        right = jnp.full((h, CHUNK, (Q_CHUNKS - 1 - c) * CHUNK), MASKED, F32)
        rows.append(jnp.concatenate([left, bias, right], axis=2))
    return jnp.concatenate(rows, axis=1)


def _pad_lanes(v, width):
    return jnp.pad(v.reshape(1, -1), ((0, 0), (0, width - v.shape[-1])))


def _layer_weights(ln_g, w_in, conv_w, a_log, dt_bias, gdn_norm_g, q_norm_g, k_norm_g, w_out, w_ple_proj,
                   ple_norm_g, w_ple_gate):
    ab0 = QKV_A
    ab1 = QKV_A + 2 * N_HEADS
    w_r = jnp.concatenate([w_in[:, :ab0], w_in[:, ab1:],
                           jnp.pad(w_in[:, ab0:ab1], ((0, 0), (0, AB_PAD - 2 * N_HEADS)))], axis=1)
    head = jnp.arange(W_GRP) // D_HEAD
    return dict(
        ln_g=ln_g.reshape(1, -1), w_r=w_r.astype(BF16),
        qg=jnp.tile(q_norm_g, N_HEADS).reshape(1, -1), kg=jnp.tile(k_norm_g, N_HEADS).reshape(1, -1),
        bd=(head[:, None] == head[None, :]).astype(BF16),
        cw=conv_w, alog=_pad_lanes(a_log, AB_PAD), dtb=_pad_lanes(dt_bias, AB_PAD),
        ng=jnp.tile(gdn_norm_g, N_HEADS).reshape(1, -1),
        wo=w_out.astype(BF16), wp=w_ple_proj.astype(BF16), pg=ple_norm_g.reshape(1, -1),
        wg=w_ple_gate.astype(BF16))


def _state_to_lanes(s):
    b = s.shape[0]
    return s.transpose(0, 2, 1, 3).reshape(b, D_HEAD, W_GRP)


def _state_from_lanes(s):
    b = s.shape[0]
    return s.reshape(b, D_HEAD, N_HEADS, D_HEAD).transpose(0, 2, 1, 3)


def _layer(h, p_i, conv0, s0, caches, bias, w, tm, tt_in):
    qkv, ab, ga, q, k, v, gb, kst, vst = _inproj(h, w["ln_g"], w["w_r"], w["qg"], w["kg"], w["bd"], tm)
    if caches is not None:
        nb = caches[0].shape[0]
        regroup = lambda a: a.reshape(nb, a.shape[1] // nb, a.shape[-1])
        qkv, ab, ga, q, k, v, gb = map(regroup, (qkv, ab, ga, q, k, v, gb))
    oa, conv_new, s_new = _gdn(qkv, ab, ga, conv0, _state_to_lanes(s0), w["cw"], w["alog"], w["dtb"], w["ng"],
                               w["bd"], tt_in)
    if caches is None:
        ob = _attn_prompt(q, k, v, gb, bias)
    else:
        ob = _attn_sample(q, k, v, caches[0], caches[1], gb, bias)
        oa, ob = oa.reshape(h.shape[0], h.shape[1], W_GRP), ob.reshape(h.shape[0], h.shape[1], W_GRP)
    y = _outproj(h, oa, ob, p_i, w["wo"], w["wp"], w["pg"], w["wg"], tm)
    return y, conv_new[:, CONV_ROWS - (CONV_W - 1):], _state_from_lanes(s_new), kst, vst


def kernel(x_prompt, x_sample, state_conv, state_gdn, cache_k, cache_v, p_prompt, p_sample, ln_g, w_in, conv_w, a_log, dt_bias, gdn_norm_g, q_norm_g, k_norm_g, rel_bias, w_out, w_ple_proj, ple_norm_g, w_ple_gate):
    depth = ln_g.shape[0]
    bp, tp, d = x_prompt.shape
    bs, ts, _ = x_sample.shape
    wc = cache_k.shape[2]
    assert tp % BAND_PAST == 0 and ts <= CHUNK and ts % CONV_ROWS == 0
    hp = x_prompt
    hs = x_sample.reshape(1, bs * ts, d)
    outs = [[] for _ in range(8)]
    for i in range(depth):
        w = _layer_weights(ln_g[i], w_in[i], conv_w[i], a_log[i], dt_bias[i], gdn_norm_g[i], q_norm_g[i],
                           k_norm_g[i], w_out[i], w_ple_proj[i], ple_norm_g[i], w_ple_gate[i])
        conv0 = jnp.zeros((bp, CONV_ROWS, QKV_A), F32)
        s0 = jnp.zeros((bp, N_HEADS, D_HEAD, D_HEAD), F32)
        bias_p = _group_bias(_rel_bias_table(rel_bias[i], BAND_PAST, CHUNK, BAND_PAST + CHUNK))
        hp, c_p, g_p, k_p, v_p = _layer(hp, p_prompt[i], conv0, s0, None, bias_p, w,
                                        tm=BAND_PAST, tt_in=4 * CHUNK)
        conv0_s = jnp.pad(state_conv[i], ((0, 0), (CONV_ROWS - (CONV_W - 1), 0), (0, 0)))
        caches = (cache_k[i].reshape(bs, wc, W_GRP), cache_v[i].reshape(bs, wc, W_GRP))
        bias_s = _rel_bias_table(rel_bias[i], wc, ts, wc + ts)
        hs, c_s, g_s, k_s, v_s = _layer(hs, p_sample[i].reshape(1, bs * ts, -1), conv0_s, state_gdn[i],
                                        caches, bias_s, w, tm=bs * ts, tt_in=ts)
        new = (c_p, g_p, k_p.reshape(bp, BAND_PAST, N_HEADS, D_HEAD), v_p.reshape(bp, BAND_PAST, N_HEADS, D_HEAD),
               c_s, g_s, k_s.reshape(bs, ts, N_HEADS, D_HEAD), v_s.reshape(bs, ts, N_HEADS, D_HEAD))
        for lst, a in zip(outs, new):
            lst.append(a)
    return (hp, hs.reshape(bs, ts, d)) + tuple(jnp.stack(lst) for lst in outs)
```

```python
import functools

import numpy as np

import jax
import jax.numpy as jnp
from jax import lax
from jax.experimental import pallas as pl
from jax.experimental.pallas import tpu as pltpu

F32 = jnp.float32
BF16 = jnp.bfloat16
EPS = 1e-6

CHUNK = 64
N_HEADS = 8
D_HEAD = 64
W_GRP = N_HEADS * D_HEAD
QKV_A = 3 * W_GRP
CONV_W = 4
BAND_PAST = 8 * CHUNK
MAX_REL = 128
AB_PAD = 128
CONV_ROWS = 8
V7X_VMEM_LIMIT = 52 * 1024 * 1024
MXU_W = 256
MXU_HEADS = MXU_W // D_HEAD
N_DOUBLINGS = 5
Q_CHUNKS = 4
MASKED = -1e30

_OFF_ZA = QKV_A
_OFF_QB = _OFF_ZA + W_GRP
_OFF_KB = _OFF_QB + W_GRP
_OFF_VB = _OFF_KB + W_GRP
_OFF_ZB = _OFF_VB + W_GRP
_OFF_AB = _OFF_ZB + W_GRP
_D_PROJ_R = _OFF_AB + AB_PAD


def _dot(a, b):
    return jnp.dot(a.astype(BF16), b.astype(BF16), preferred_element_type=F32)


def _dot_split(x, w_bf16, n_terms):
    terms, r = [], x
    for _ in range(n_terms):
        t = r.astype(BF16)
        terms.append(t)
        r = r - t.astype(F32)
    acc = None
    for t in reversed(terms):
        d = jnp.dot(t, w_bf16, preferred_element_type=F32)
        acc = d if acc is None else acc + d
    return acc


def _sigmoid(x):
    return 1.0 / (1.0 + jnp.exp(-x))


def _silu(x):
    return x * _sigmoid(x)


def _softplus(x):
    return jnp.maximum(x, 0.0) + jnp.log1p(jnp.exp(-jnp.abs(x)))


def _head_mean_square(y, blockdiag_ones):
    return jnp.dot((y * y).astype(BF16), blockdiag_ones, preferred_element_type=F32) * (1.0 / D_HEAD)


def _inproj_kernel(x_ref, lng_ref, w_ref, qg_ref, kg_ref, bd_ref,
                   qkv_ref, ab_ref, ga_ref, q_ref, k_ref, v_ref, gb_ref, kst_ref, vst_ref):
    x = x_ref[0]
    xn = x * lax.rsqrt(jnp.mean(x * x, axis=-1, keepdims=True) + EPS) * lng_ref[...]
    xb = xn.astype(BF16)

    def proj(lo, hi):
        return jnp.dot(xb, w_ref[:, lo:hi], preferred_element_type=F32)

    bd = bd_ref[...]
    qkv_ref[0] = proj(0, QKV_A)
    ga_ref[0] = _silu(proj(_OFF_ZA, _OFF_QB))
    qb = proj(_OFF_QB, _OFF_KB)
    kb = proj(_OFF_KB, _OFF_VB)
    ms = _head_mean_square(jnp.concatenate([qb, kb], axis=0), bd)
    qn = qb * lax.rsqrt(ms[:qb.shape[0]] + EPS) * qg_ref[...]
    q_ref[0] = (qn * (D_HEAD ** -0.5)).astype(BF16)
    kn = kb * lax.rsqrt(ms[qb.shape[0]:] + EPS) * kg_ref[...]
    k_ref[0] = kn.astype(BF16)
    kst_ref[0] = kn
    vb = proj(_OFF_VB, _OFF_ZB)
    v_ref[0] = vb.astype(BF16)
    vst_ref[0] = vb
    gb_ref[0] = _silu(proj(_OFF_ZB, _OFF_AB))
    ab_ref[0] = proj(_OFF_AB, _D_PROJ_R)


def _inproj(x, ln_g, w_r, qg, kg, bd, tm):
    b, t, d = x.shape
    assert t % tm == 0
    grid = (b, t // tm)
    row = lambda w: pl.BlockSpec((1, tm, w), lambda i, j: (i, j, 0))
    const = lambda s: pl.BlockSpec(s, lambda i, j: (0,) * len(s))
    last = pl.BlockSpec((1, tm, W_GRP), lambda i, j: (i, 0, 0))
    sds = lambda w, dt: jax.ShapeDtypeStruct((b, t, w), dt)
    return pl.pallas_call(
        _inproj_kernel,
        grid=grid,
        in_specs=[row(d), const((1, d)), const((d, _D_PROJ_R)), const((1, W_GRP)), const((1, W_GRP)),
                  const((W_GRP, W_GRP))],
        out_specs=[row(QKV_A), row(AB_PAD), row(W_GRP), row(W_GRP), row(W_GRP), row(W_GRP), row(W_GRP),
                   last, last],
        out_shape=[sds(QKV_A, F32), sds(AB_PAD, F32), sds(W_GRP, F32), sds(W_GRP, BF16), sds(W_GRP, BF16),
                   sds(W_GRP, BF16), sds(W_GRP, F32),
                   jax.ShapeDtypeStruct((b, tm, W_GRP), F32), jax.ShapeDtypeStruct((b, tm, W_GRP), F32)],
        compiler_params=pltpu.CompilerParams(dimension_semantics=("arbitrary", "arbitrary"),
                                             vmem_limit_bytes=V7X_VMEM_LIMIT),
        name="inproj",
    )(x, ln_g, w_r, qg, kg, bd)


def _tile_rows(x, n):
    return jnp.concatenate([x] * n, axis=0)


def _gdn_tile(q_s, k_s, v_s, gcx_s, bex_s, o_s, s_ref, bdm_ref, bdmf_ref, lvl_ref, n_chunks):
    row = lax.broadcasted_iota(jnp.int32, (CHUNK, MXU_W), 0)
    col = jnp.bitwise_and(lax.broadcasted_iota(jnp.int32, (CHUNK, MXU_W), 1), D_HEAD - 1)
    incl = row >= col
    strict = row > col
    diag = row == col
    pair = strict & (jnp.right_shift(row, 1) == jnp.right_shift(col, 1))
    bdm = bdm_ref[...]
    blockdiag = lambda x: _tile_rows(x.astype(BF16), MXU_HEADS) * bdm
    mm = lambda a, b: jnp.dot(a, b, preferred_element_type=F32)

    chains = []
    for c in range(n_chunks):
        rows = slice(c * CHUNK, (c + 1) * CHUNK)
        for grp in range(N_HEADS // MXU_HEADS):
            ls = slice(grp * MXU_W, (grp + 1) * MXU_W)
            qc, kc, vc, gcx, bex = (r[rows, ls] for r in (q_s, k_s, v_s, gcx_s, bex_s))
            gc_row = jnp.sum(jnp.where(diag, gcx, 0.0), axis=0, keepdims=True)
            gc_last = gcx[CHUNK - 1:CHUNK, :]
            egc = jnp.exp(gcx)
            k16 = kc.astype(BF16)
            chains.append(dict(
                rows=rows, ls=ls, bex=bex, k16=k16,
                decay=jnp.where(incl, jnp.exp(jnp.where(incl, gcx - gc_row, 0.0)), 0.0),
                g_tot=jnp.exp(gc_last), u_rhs=vc * bex, w_rhs=kc * (bex * egc),
                qk_lhs=jnp.concatenate([qc.astype(BF16), k16], axis=0),
                q_dec=(qc * egc).astype(BF16), k_tail=(kc * jnp.exp(gc_last - gcx)).astype(BF16)))
    for ch in chains:
        ch["sc"] = lax.dot_general(ch["qk_lhs"], blockdiag(ch["k16"]), (((1,), (1,)), ((), ())),
                                   preferred_element_type=F32)
    for ch in chains:
        sc = ch.pop("sc")
        ch["qk"] = (sc[0:CHUNK] * ch["decay"]).astype(BF16)
        a_low = jnp.where(strict, ch["bex"] * sc[CHUNK:2 * CHUNK] * ch["decay"], 0.0)
        ch["a_rows"] = _tile_rows(a_low.astype(BF16), MXU_HEADS)
        ch["dinv"] = jnp.where(diag, 1.0, 0.0) - jnp.where(pair, a_low, 0.0)
    for li in range(N_DOUBLINGS):
        for ch in chains:
            ch["t1"] = mm(ch["dinv"].astype(BF16), ch["a_rows"] * lvl_ref[li])
        for ch in chains:
            ch["dinv"] = ch["dinv"] - mm(ch.pop("t1").astype(BF16), blockdiag(ch["dinv"]))
    for ch in chains:
        d16 = ch["dinv"].astype(BF16)
        ch["u"] = mm(d16, blockdiag(ch["u_rhs"]))
        ch["w"] = mm(d16, blockdiag(ch["w_rhs"]))
    for ch in chains:
        rows, ls = ch["rows"], ch["ls"]
        s = s_ref[:, ls]
        wq = mm(jnp.concatenate([ch["w"].astype(BF16), ch["q_dec"]], axis=0), blockdiag(s))
        v16 = (ch["u"] - wq[0:CHUNK]).astype(BF16)
        o_s[rows, ls] = wq[CHUNK:2 * CHUNK] + mm(ch["qk"], _tile_rows(v16, MXU_HEADS) * bdm)
        r = lax.dot_general(ch["k_tail"], v16, (((0,), (0,)), ((), ())),
                            preferred_element_type=F32) * bdmf_ref[...]
        s_ref[:, ls] = s * ch["g_tot"] + ((r[0:D_HEAD] + r[D_HEAD:2 * D_HEAD])
                                          + (r[2 * D_HEAD:3 * D_HEAD] + r[3 * D_HEAD:4 * D_HEAD]))


def _gdn_kernel(qkv_ref, ab_ref, ga_ref, conv0_ref, s0_ref, cw_ref, alog_ref, dtb_ref, ng_ref, bdones_ref,
                ea_ref, eb_ref, bdm_ref, bdmf_ref, lvl_ref,
                o_ref, convout_ref, sout_ref,
                xcat_ref, q_s, k_s, v_s, gcx_s, bex_s, o_s, s_ref, *, tt_in, tt):
    t = pl.program_id(1)

    @pl.when(t == 0)
    def _():
        xcat_ref[0:CONV_ROWS, :] = conv0_ref[0]
        s_ref[...] = s0_ref[0]

    xcat_ref[CONV_ROWS:CONV_ROWS + tt_in, :] = qkv_ref[0]
    if tt > tt_in:
        xcat_ref[CONV_ROWS + tt_in:CONV_ROWS + tt, :] = jnp.zeros((tt - tt_in, QKV_A), F32)
    cw = cw_ref[...]
    first = CONV_ROWS - (CONV_W - 1)
    conv = xcat_ref[first:first + tt, :] * cw[0:1, :]
    for j in range(1, CONV_W):
        conv = conv + xcat_ref[first + j:first + j + tt, :] * cw[j:j + 1, :]
    carry = xcat_ref[tt_in:tt_in + CONV_ROWS, :]
    convout_ref[0] = carry
    xcat_ref[0:CONV_ROWS, :] = carry

    c = _silu(conv)
    bdones = bdones_ref[...]
    q = c[:, 0:W_GRP]
    k = c[:, W_GRP:2 * W_GRP]
    ss = _head_mean_square(jnp.concatenate([q, k], axis=0), bdones) * D_HEAD
    q_s[...] = q * lax.rsqrt(ss[:tt] + EPS) * (D_HEAD ** -0.5)
    k_s[...] = k * lax.rsqrt(ss[tt:] + EPS)
    v_s[...] = c[:, 2 * W_GRP:3 * W_GRP]

    ab = ab_ref[0]
    g = -jnp.exp(alog_ref[...]) * _softplus(ab + dtb_ref[...])
    beta = _sigmoid(ab)
    if tt > tt_in:
        pad = jnp.zeros((tt - tt_in, AB_PAD), F32)
        g = jnp.concatenate([g, pad], axis=0)
        beta = jnp.concatenate([beta, pad], axis=0)
    rin = jnp.bitwise_and(lax.broadcasted_iota(jnp.int32, (tt, AB_PAD), 0), CHUNK - 1)
    gc = g
    step = 1
    while step < CHUNK:
        gc = gc + jnp.where(rin >= step, pltpu.roll(gc, step, 0), 0.0)
        step *= 2
    gcx_s[...] = _dot_split(gc, ea_ref[...], 3)
    bex_s[...] = _dot_split(beta, eb_ref[...], 2)

    _gdn_tile(q_s, k_s, v_s, gcx_s, bex_s, o_s, s_ref, bdm_ref, bdmf_ref, lvl_ref, tt // CHUNK)

    o = o_s[0:tt_in, :]
    o_ref[0] = o * lax.rsqrt(_head_mean_square(o, bdones) + EPS) * ng_ref[...] * ga_ref[0]

    @pl.when(t == pl.num_programs(1) - 1)
    def _():
        sout_ref[0] = s_ref[...]


def _gdn_masks():
    idx = np.arange(MXU_W)
    head, pos = idx // D_HEAD, idx % D_HEAD
    bdm = head[:, None] == head[None, :]
    levels = []
    for li in range(N_DOUBLINGS):
        n = 2 << li
        pr, pc = pos[:, None], pos[None, :]
        levels.append(bdm & (pr // (2 * n) == pc // (2 * n)) & (pr // n != pc // n))
    expand = np.arange(W_GRP)[None, :] // D_HEAD == np.arange(AB_PAD)[:, None]
    return dict(bdm=jnp.asarray(bdm, BF16), bdmf=jnp.asarray(bdm, F32),
                lvl=jnp.asarray(np.stack(levels), BF16),
                ea=jnp.asarray(expand, BF16), eb=jnp.asarray(np.roll(expand, N_HEADS, axis=0), BF16))


def _gdn(qkv, ab, ga, conv0, s0, cw, alog, dtb, ng, bdones, tt_in):
    b, t, _ = qkv.shape
    tt = -(-tt_in // CHUNK) * CHUNK
    assert t % tt_in == 0 and tt_in >= CONV_ROWS and (tt == tt_in or t == tt_in)
    m = _gdn_masks()
    row = lambda w: pl.BlockSpec((1, tt_in, w), lambda i, j: (i, j, 0))
    const = lambda s: pl.BlockSpec(s, lambda i, j: (0,) * len(s))
    per_b = lambda s: pl.BlockSpec((1,) + s, lambda i, j: (i,) + (0,) * len(s))
    return pl.pallas_call(
        functools.partial(_gdn_kernel, tt_in=tt_in, tt=tt),
        grid=(b, t // tt_in),
        in_specs=[row(QKV_A), row(AB_PAD), row(W_GRP), per_b((CONV_ROWS, QKV_A)), per_b((D_HEAD, W_GRP)),
                  const((CONV_W, QKV_A)), const((1, AB_PAD)), const((1, AB_PAD)), const((1, W_GRP)),
                  const((W_GRP, W_GRP)), const((AB_PAD, W_GRP)), const((AB_PAD, W_GRP)),
                  const((MXU_W, MXU_W)), const((MXU_W, MXU_W)), const((N_DOUBLINGS, MXU_W, MXU_W))],
        out_specs=[row(W_GRP), per_b((CONV_ROWS, QKV_A)), per_b((D_HEAD, W_GRP))],
        out_shape=[jax.ShapeDtypeStruct((b, t, W_GRP), F32),
                   jax.ShapeDtypeStruct((b, CONV_ROWS, QKV_A), F32),
                   jax.ShapeDtypeStruct((b, D_HEAD, W_GRP), F32)],
        scratch_shapes=[pltpu.VMEM((tt + CONV_ROWS, QKV_A), F32)]
        + [pltpu.VMEM((tt, W_GRP), F32)] * 6 + [pltpu.VMEM((D_HEAD, W_GRP), F32)],
        compiler_params=pltpu.CompilerParams(dimension_semantics=("arbitrary", "arbitrary"),
                                             vmem_limit_bytes=V7X_VMEM_LIMIT),
        name="gdn",
    )(qkv, ab, ga, conv0, s0, cw, alog, dtb, ng, bdones, m["ea"], m["eb"], m["bdm"], m["bdmf"], m["lvl"])


def _attend_heads(q, kw, vw, bias_ref, first_valid):
    lq, span = q.shape[0], kw.shape[0]
    col = lax.broadcasted_iota(jnp.int32, (lq, span), 1)
    valid = col >= first_valid
    head = lambda h: slice(h * D_HEAD, (h + 1) * D_HEAD)
    scores = lambda h: lax.dot_general(q[:, head(h)], kw[:, head(h)], (((1,), (1,)), ((), ())),
                                       preferred_element_type=F32)
    outs = []
    ahead = 3
    pending = [scores(h) for h in range(ahead)]
    for h in range(N_HEADS):
        s = pending.pop(0)
        if h + ahead < N_HEADS:
            pending.append(scores(h + ahead))
        s = jnp.where(valid, s + bias_ref[h], -jnp.inf)
        m = jnp.max(s, axis=-1, keepdims=True)
        p = jnp.exp(s - m)
        l = jnp.sum(p, axis=-1, keepdims=True)
        o = jnp.dot(p.astype(BF16), vw[:, head(h)], preferred_element_type=F32)
        outs.append(o / l)
    return jnp.concatenate(outs, axis=1)


def _attn_prompt_kernel(q_ref, kp_ref, kc_ref, vp_ref, vc_ref, gb_ref, bias_ref, o_ref, kwin_ref, vwin_ref):
    j = pl.program_id(1)
    tq = BAND_PAST
    kwin_ref[0:tq, :] = kp_ref[0]
    kwin_ref[tq:2 * tq, :] = kc_ref[0]
    vwin_ref[0:tq, :] = vp_ref[0]
    vwin_ref[tq:2 * tq, :] = vc_ref[0]

    def chunk_group(c, carry):
        r0 = pl.multiple_of(c * (Q_CHUNKS * CHUNK), Q_CHUNKS * CHUNK)
        rows = pl.ds(r0, Q_CHUNKS * CHUNK)
        win = pl.ds(r0, BAND_PAST + Q_CHUNKS * CHUNK)
        first_valid = jnp.where(j == 0, BAND_PAST - r0, 0)
        o = _attend_heads(q_ref[0, rows, :], kwin_ref[win, :], vwin_ref[win, :], bias_ref, first_valid)
        o_ref[0, rows, :] = o * gb_ref[0, rows, :]
        return carry

    lax.fori_loop(0, tq // (Q_CHUNKS * CHUNK), chunk_group, 0)


def _attn_prompt(q, k, v, gb, bias):
    b, t, _ = q.shape
    tq = BAND_PAST
    assert t % tq == 0
    cur = pl.BlockSpec((1, tq, W_GRP), lambda i, j: (i, j, 0))
    prev = pl.BlockSpec((1, tq, W_GRP), lambda i, j: (i, jnp.maximum(j - 1, 0), 0))
    return pl.pallas_call(
        _attn_prompt_kernel,
        grid=(b, t // tq),
        in_specs=[cur, prev, cur, prev, cur, cur,
                  pl.BlockSpec(bias.shape, lambda i, j: (0, 0, 0))],
        out_specs=cur,
        out_shape=jax.ShapeDtypeStruct((b, t, W_GRP), F32),
        scratch_shapes=[pltpu.VMEM((2 * tq, W_GRP), BF16), pltpu.VMEM((2 * tq, W_GRP), BF16)],
        compiler_params=pltpu.CompilerParams(dimension_semantics=("arbitrary", "arbitrary"),
                                             vmem_limit_bytes=V7X_VMEM_LIMIT),
        name="attn_prompt",
    )(q, k, k, v, v, gb, bias)


def _attn_sample_kernel(q_ref, kn_ref, vn_ref, ck_ref, cv_ref, gb_ref, bias_ref, o_ref):
    kw = jnp.concatenate([ck_ref[0].astype(BF16), kn_ref[0]], axis=0)
    vw = jnp.concatenate([cv_ref[0].astype(BF16), vn_ref[0]], axis=0)
    o = _attend_heads(q_ref[0], kw, vw, bias_ref, 0)
    o_ref[0] = o * gb_ref[0]


def _attn_sample(q, k, v, ck, cv, gb, bias):
    b, tn, _ = q.shape
    wc = ck.shape[1]
    new = pl.BlockSpec((1, tn, W_GRP), lambda i: (i, 0, 0))
    old = pl.BlockSpec((1, wc, W_GRP), lambda i: (i, 0, 0))
    return pl.pallas_call(
        _attn_sample_kernel,
        grid=(b,),
        in_specs=[new, new, new, old, old, new, pl.BlockSpec((N_HEADS, tn, wc + tn), lambda i: (0, 0, 0))],
        out_specs=new,
        out_shape=jax.ShapeDtypeStruct((b, tn, W_GRP), F32),
        compiler_params=pltpu.CompilerParams(dimension_semantics=("arbitrary",),
                                             vmem_limit_bytes=V7X_VMEM_LIMIT),
        name="attn_sample",
    )(q, k, v, ck, cv, gb, bias)


def _outproj_kernel(x_ref, oa_ref, ob_ref, p_ref, wo_ref, wp_ref, pg_ref, wg_ref, y_ref):
    h = x_ref[0] + (_dot(oa_ref[0], wo_ref[0:W_GRP, :]) + _dot(ob_ref[0], wo_ref[W_GRP:2 * W_GRP, :]))
    e = _dot(p_ref[0], wp_ref[...])
    e = e * lax.rsqrt(jnp.mean(e * e, axis=-1, keepdims=True) + EPS) * pg_ref[...]
    y_ref[0] = h + _sigmoid(_dot(h, wg_ref[...])) * e


def _outproj(x, oa, ob, p, wo, wp, pg, wg, tm):
    b, t, d = x.shape
    pd = p.shape[-1]
    assert t % tm == 0
    row = lambda w: pl.BlockSpec((1, tm, w), lambda i, j: (i, j, 0))
    const = lambda s: pl.BlockSpec(s, lambda i, j: (0,) * len(s))
    return pl.pallas_call(
        _outproj_kernel,
        grid=(b, t // tm),
        in_specs=[row(d), row(W_GRP), row(W_GRP), row(pd), const((2 * W_GRP, d)), const((pd, d)),
                  const((1, d)), const((d, d))],
        out_specs=row(d),
        out_shape=jax.ShapeDtypeStruct((b, t, d), F32),
        compiler_params=pltpu.CompilerParams(dimension_semantics=("arbitrary", "arbitrary"),
                                             vmem_limit_bytes=V7X_VMEM_LIMIT),
        name="outproj",
    )(x, oa, ob, p, wo, wp, pg, wg)


def _rel_bias_table(table, q0, nq, nk):
    n = nk + nq - 1
    m = (q0 + nq - 1) - np.arange(n + 1)
    rev = table[:, np.clip(m, -MAX_REL, MAX_REL) + MAX_REL].astype(F32)
    skew = jnp.tile(rev, (1, nq))[:, :nq * n].reshape(-1, nq, n)
    return skew[:, :, nq - 1:nq - 1 + nk]


def _group_bias(bias):
    h = bias.shape[0]
    rows = []
    for c in range(Q_CHUNKS):
        left = jnp.full((h, CHUNK, c * CHUNK), MASKED, F32)
        right = jnp.full((h, CHUNK, (Q_CHUNKS - 1 - c) * CHUNK), MASKED, F32)
        rows.append(jnp.concatenate([left, bias, right], axis=2))
    return jnp.concatenate(rows, axis=1)


def _pad_lanes(v, width):
    return jnp.pad(v.reshape(1, -1), ((0, 0), (0, width - v.shape[-1])))


def _layer_weights(ln_g, w_in, conv_w, a_log, dt_bias, gdn_norm_g, q_norm_g, k_norm_g, w_out, w_ple_proj,
                   ple_norm_g, w_ple_gate):
    ab0 = QKV_A
    ab1 = QKV_A + 2 * N_HEADS
    w_r = jnp.concatenate([w_in[:, :ab0], w_in[:, ab1:],
                           jnp.pad(w_in[:, ab0:ab1], ((0, 0), (0, AB_PAD - 2 * N_HEADS)))], axis=1)
    head = jnp.arange(W_GRP) // D_HEAD
    return dict(
        ln_g=ln_g.reshape(1, -1), w_r=w_r.astype(BF16),
        qg=jnp.tile(q_norm_g, N_HEADS).reshape(1, -1), kg=jnp.tile(k_norm_g, N_HEADS).reshape(1, -1),
        bd=(head[:, None] == head[None, :]).astype(BF16),
        cw=conv_w, alog=_pad_lanes(a_log, AB_PAD), dtb=_pad_lanes(dt_bias, AB_PAD),
        ng=jnp.tile(gdn_norm_g, N_HEADS).reshape(1, -1),
        wo=w_out.astype(BF16), wp=w_ple_proj.astype(BF16), pg=ple_norm_g.reshape(1, -1),
        wg=w_ple_gate.astype(BF16))


def _state_to_lanes(s):
    b = s.shape[0]
    return s.transpose(0, 2, 1, 3).reshape(b, D_HEAD, W_GRP)


def _state_from_lanes(s):
    b = s.shape[0]
    return s.reshape(b, D_HEAD, N_HEADS, D_HEAD).transpose(0, 2, 1, 3)


def _layer(h, p_i, conv0, s0, caches, bias, w, tm, tt_in):
    qkv, ab, ga, q, k, v, gb, kst, vst = _inproj(h, w["ln_g"], w["w_r"], w["qg"], w["kg"], w["bd"], tm)
    if caches is not None:
        nb = caches[0].shape[0]
        regroup = lambda a: a.reshape(nb, a.shape[1] // nb, a.shape[-1])
        qkv, ab, ga, q, k, v, gb = map(regroup, (qkv, ab, ga, q, k, v, gb))
    oa, conv_new, s_new = _gdn(qkv, ab, ga, conv0, _state_to_lanes(s0), w["cw"], w["alog"], w["dtb"], w["ng"],
                               w["bd"], tt_in)
    if caches is None:
        ob = _attn_prompt(q, k, v, gb, bias)
    else:
        ob = _attn_sample(q, k, v, caches[0], caches[1], gb, bias)
        oa, ob = oa.reshape(h.shape[0], h.shape[1], W_GRP), ob.reshape(h.shape[0], h.shape[1], W_GRP)
    y = _outproj(h, oa, ob, p_i, w["wo"], w["wp"], w["pg"], w["wg"], tm)
    return y, conv_new[:, CONV_ROWS - (CONV_W - 1):], _state_from_lanes(s_new), kst, vst


def kernel(x_prompt, x_sample, state_conv, state_gdn, cache_k, cache_v, p_prompt, p_sample, ln_g, w_in, conv_w, a_log, dt_bias, gdn_norm_g, q_norm_g, k_norm_g, rel_bias, w_out, w_ple_proj, ple_norm_g, w_ple_gate):
    depth = ln_g.shape[0]
    bp, tp, d = x_prompt.shape
    bs, ts, _ = x_sample.shape
    wc = cache_k.shape[2]
    assert tp % BAND_PAST == 0 and ts <= CHUNK and ts % CONV_ROWS == 0
    hp = x_prompt
    hs = x_sample.reshape(1, bs * ts, d)
    outs = [[] for _ in range(8)]
    for i in range(depth):
        w = _layer_weights(ln_g[i], w_in[i], conv_w[i], a_log[i], dt_bias[i], gdn_norm_g[i], q_norm_g[i],
                           k_norm_g[i], w_out[i], w_ple_proj[i], ple_norm_g[i], w_ple_gate[i])
        conv0 = jnp.zeros((bp, CONV_ROWS, QKV_A), F32)
        s0 = jnp.zeros((bp, N_HEADS, D_HEAD, D_HEAD), F32)
        bias_p = _group_bias(_rel_bias_table(rel_bias[i], BAND_PAST, CHUNK, BAND_PAST + CHUNK))
        hp, c_p, g_p, k_p, v_p = _layer(hp, p_prompt[i], conv0, s0, None, bias_p, w,
                                        tm=BAND_PAST, tt_in=4 * CHUNK)
        conv0_s = jnp.pad(state_conv[i], ((0, 0), (CONV_ROWS - (CONV_W - 1), 0), (0, 0)))
        caches = (cache_k[i].reshape(bs, wc, W_GRP), cache_v[i].reshape(bs, wc, W_GRP))
        bias_s = _rel_bias_table(rel_bias[i], wc, ts, wc + ts)
        hs, c_s, g_s, k_s, v_s = _layer(hs, p_sample[i].reshape(1, bs * ts, -1), conv0_s, state_gdn[i],
                                        caches, bias_s, w, tm=bs * ts, tt_in=ts)
        new = (c_p, g_p, k_p.reshape(bp, BAND_PAST, N_HEADS, D_HEAD), v_p.reshape(bp, BAND_PAST, N_HEADS, D_HEAD),
               c_s, g_s, k_s.reshape(bs, ts, N_HEADS, D_HEAD), v_s.reshape(bs, ts, N_HEADS, D_HEAD))
        for lst, a in zip(outs, new):
            lst.append(a)
    stack = lambda lst: lst[0][None] if len(lst) == 1 else jnp.stack(lst)
    return (hp, hs.reshape(bs, ts, d)) + tuple(stack(lst) for lst in outs)
```

```python
import functools

import numpy as np

import jax
import jax.numpy as jnp
from jax import lax
from jax.experimental import pallas as pl
from jax.experimental.pallas import tpu as pltpu

F32 = jnp.float32
BF16 = jnp.bfloat16
EPS = 1e-6

CHUNK = 64
N_HEADS = 8
D_HEAD = 64
W_GRP = N_HEADS * D_HEAD
QKV_A = 3 * W_GRP
CONV_W = 4
BAND_PAST = 8 * CHUNK
MAX_REL = 128
AB_PAD = 128
CONV_ROWS = 8
V7X_VMEM_LIMIT = 52 * 1024 * 1024
MXU_W = 256
MXU_HEADS = MXU_W // D_HEAD
N_DOUBLINGS = 5
Q_CHUNKS = 4
MASKED = -1e30

_OFF_ZA = QKV_A
_OFF_QB = _OFF_ZA + W_GRP
_OFF_KB = _OFF_QB + W_GRP
_OFF_VB = _OFF_KB + W_GRP
_OFF_ZB = _OFF_VB + W_GRP
_OFF_AB = _OFF_ZB + W_GRP
_D_PROJ_R = _OFF_AB + AB_PAD


def _dot(a, b):
    return jnp.dot(a.astype(BF16), b.astype(BF16), preferred_element_type=F32)


def _dot_split(x, w_bf16, n_terms):
    terms, r = [], x
    for _ in range(n_terms):
        t = r.astype(BF16)
        terms.append(t)
        r = r - t.astype(F32)
    acc = None
    for t in reversed(terms):
        d = jnp.dot(t, w_bf16, preferred_element_type=F32)
        acc = d if acc is None else acc + d
    return acc


def _sigmoid(x):
    return 1.0 / (1.0 + jnp.exp(-x))


def _silu(x):
    return x * _sigmoid(x)


def _softplus(x):
    return jnp.maximum(x, 0.0) + jnp.log1p(jnp.exp(-jnp.abs(x)))


def _head_mean_square(y, blockdiag_ones):
    return jnp.dot((y * y).astype(BF16), blockdiag_ones, preferred_element_type=F32) * (1.0 / D_HEAD)


def _inproj_kernel(x_ref, lng_ref, w_ref, qg_ref, kg_ref, bd_ref, conv0_ref, cw_ref,
                   qkv_ref, ab_ref, ga_ref, q_ref, k_ref, v_ref, gb_ref, kst_ref, vst_ref, convout_ref,
                   xcat_ref):
    tm = x_ref.shape[1]
    x = x_ref[0]
    xn = x * lax.rsqrt(jnp.mean(x * x, axis=-1, keepdims=True) + EPS) * lng_ref[...]
    xb = xn.astype(BF16)

    def proj(lo, hi):
        return jnp.dot(xb, w_ref[:, lo:hi], preferred_element_type=F32)

    @pl.when(pl.program_id(1) == 0)
    def _():
        xcat_ref[0:CONV_ROWS, :] = conv0_ref[0]

    xcat_ref[CONV_ROWS:CONV_ROWS + tm, :] = proj(0, QKV_A)
    cw = cw_ref[...]
    first = CONV_ROWS - (CONV_W - 1)

    def conv_block(c0):
        cols = slice(c0, c0 + W_GRP)
        conv = xcat_ref[first:first + tm, cols] * cw[0:1, cols]
        for j in range(1, CONV_W):
            conv = conv + xcat_ref[first + j:first + j + tm, cols] * cw[j:j + 1, cols]
        qkv_ref[0, :, cols] = _silu(conv)

    bd = bd_ref[...]
    za = proj(_OFF_ZA, _OFF_QB)
    conv_block(0)
    ga_ref[0] = _silu(za)
    qb = proj(_OFF_QB, _OFF_KB)
    kb = proj(_OFF_KB, _OFF_VB)
    conv_block(W_GRP)
    ms = _head_mean_square(jnp.concatenate([qb, kb], axis=0), bd)
    qn = qb * lax.rsqrt(ms[:qb.shape[0]] + EPS) * qg_ref[...]
    q_ref[0] = (qn * (D_HEAD ** -0.5)).astype(BF16)
    kn = kb * lax.rsqrt(ms[qb.shape[0]:] + EPS) * kg_ref[...]
    k_ref[0] = kn.astype(BF16)
    kst_ref[0] = kn
    vb = proj(_OFF_VB, _OFF_ZB)
    conv_block(2 * W_GRP)
    v_ref[0] = vb.astype(BF16)
    vst_ref[0] = vb
    gb_ref[0] = _silu(proj(_OFF_ZB, _OFF_AB))
    ab_ref[0] = proj(_OFF_AB, _D_PROJ_R)
    carry = xcat_ref[tm:tm + CONV_ROWS, :]
    convout_ref[0] = carry
    xcat_ref[0:CONV_ROWS, :] = carry


def _inproj(x, ln_g, w_r, qg, kg, bd, conv0, cw, tm):
    b, t, d = x.shape
    assert t % tm == 0 and tm >= CONV_ROWS
    grid = (b, t // tm)
    row = lambda w: pl.BlockSpec((1, tm, w), lambda i, j: (i, j, 0))
    const = lambda s: pl.BlockSpec(s, lambda i, j: (0,) * len(s))
    last = pl.BlockSpec((1, tm, W_GRP), lambda i, j: (i, 0, 0))
    state = pl.BlockSpec((1, CONV_ROWS, QKV_A), lambda i, j: (i, 0, 0))
    sds = lambda w, dt: jax.ShapeDtypeStruct((b, t, w), dt)
    return pl.pallas_call(
        _inproj_kernel,
        grid=grid,
        in_specs=[row(d), const((1, d)), const((d, _D_PROJ_R)), const((1, W_GRP)), const((1, W_GRP)),
                  const((W_GRP, W_GRP)), state, const((CONV_W, QKV_A))],
        out_specs=[row(QKV_A), row(AB_PAD), row(W_GRP), row(W_GRP), row(W_GRP), row(W_GRP), row(W_GRP),
                   last, last, state],
        out_shape=[sds(QKV_A, F32), sds(AB_PAD, F32), sds(W_GRP, F32), sds(W_GRP, BF16), sds(W_GRP, BF16),
                   sds(W_GRP, BF16), sds(W_GRP, F32),
                   jax.ShapeDtypeStruct((b, tm, W_GRP), F32), jax.ShapeDtypeStruct((b, tm, W_GRP), F32),
                   jax.ShapeDtypeStruct((b, CONV_ROWS, QKV_A), F32)],
        scratch_shapes=[pltpu.VMEM((tm + CONV_ROWS, QKV_A), F32)],
        compiler_params=pltpu.CompilerParams(dimension_semantics=("arbitrary", "arbitrary"),
                                             vmem_limit_bytes=V7X_VMEM_LIMIT),
        name="inproj",
    )(x, ln_g, w_r, qg, kg, bd, conv0, cw)


def _tile_rows(x, n):
    return jnp.concatenate([x] * n, axis=0)


def _gdn_tile(q_s, k_s, v_s, gcx_s, bex_s, o_s, s_ref, bdm_ref, bdmf_ref, lvl_ref, n_chunks):
    row = lax.broadcasted_iota(jnp.int32, (CHUNK, MXU_W), 0)
    col = jnp.bitwise_and(lax.broadcasted_iota(jnp.int32, (CHUNK, MXU_W), 1), D_HEAD - 1)
    incl = row >= col
    strict = row > col
    diag = row == col
    pair = strict & (jnp.right_shift(row, 1) == jnp.right_shift(col, 1))
    bdm = bdm_ref[...]
    blockdiag = lambda x: _tile_rows(x.astype(BF16), MXU_HEADS) * bdm
    mm = lambda a, b: jnp.dot(a, b, preferred_element_type=F32)

    chains = []
    for c in range(n_chunks):
        rows = slice(c * CHUNK, (c + 1) * CHUNK)
        for grp in range(N_HEADS // MXU_HEADS):
            ls = slice(grp * MXU_W, (grp + 1) * MXU_W)
            qc, kc, vc, gcx, bex = (r[rows, ls] for r in (q_s, k_s, v_s, gcx_s, bex_s))
            gc_row = jnp.sum(jnp.where(diag, gcx, 0.0), axis=0, keepdims=True)
            gc_last = gcx[CHUNK - 1:CHUNK, :]
            egc = jnp.exp(gcx)
            k16 = kc.astype(BF16)
            chains.append(dict(
                rows=rows, ls=ls, bex=bex, k16=k16,
                decay=jnp.where(incl, jnp.exp(jnp.where(incl, gcx - gc_row, 0.0)), 0.0),
                g_tot=jnp.exp(gc_last), u_rhs=vc * bex, w_rhs=kc * (bex * egc),
                qk_lhs=jnp.concatenate([qc.astype(BF16), k16], axis=0),
                q_dec=(qc * egc).astype(BF16), k_tail=(kc * jnp.exp(gc_last - gcx)).astype(BF16)))
    for ch in chains:
        ch["sc"] = lax.dot_general(ch["qk_lhs"], blockdiag(ch["k16"]), (((1,), (1,)), ((), ())),
                                   preferred_element_type=F32)
    for ch in chains:
        sc = ch.pop("sc")
        ch["qk"] = (sc[0:CHUNK] * ch["decay"]).astype(BF16)
        a_low = jnp.where(strict, ch["bex"] * sc[CHUNK:2 * CHUNK] * ch["decay"], 0.0)
        ch["a_rows"] = _tile_rows(a_low.astype(BF16), MXU_HEADS)
        ch["dinv"] = jnp.where(diag, 1.0, 0.0) - jnp.where(pair, a_low, 0.0)
    for li in range(N_DOUBLINGS):
        for ch in chains:
            ch["t1"] = mm(ch["dinv"].astype(BF16), ch["a_rows"] * lvl_ref[li])
        for ch in chains:
            ch["dinv"] = ch["dinv"] - mm(ch.pop("t1").astype(BF16), blockdiag(ch["dinv"]))
    for ch in chains:
        d16 = ch["dinv"].astype(BF16)
        ch["u"] = mm(d16, blockdiag(ch["u_rhs"]))
        ch["w"] = mm(d16, blockdiag(ch["w_rhs"]))
    for ch in chains:
        rows, ls = ch["rows"], ch["ls"]
        s = s_ref[:, ls]
        wq = mm(jnp.concatenate([ch["w"].astype(BF16), ch["q_dec"]], axis=0), blockdiag(s))
        v16 = (ch["u"] - wq[0:CHUNK]).astype(BF16)
        o_s[rows, ls] = wq[CHUNK:2 * CHUNK] + mm(ch["qk"], _tile_rows(v16, MXU_HEADS) * bdm)
        r = lax.dot_general(ch["k_tail"], v16, (((0,), (0,)), ((), ())),
                            preferred_element_type=F32) * bdmf_ref[...]
        s_ref[:, ls] = s * ch["g_tot"] + ((r[0:D_HEAD] + r[D_HEAD:2 * D_HEAD])
                                          + (r[2 * D_HEAD:3 * D_HEAD] + r[3 * D_HEAD:4 * D_HEAD]))


def _gdn_kernel(qkv_ref, ab_ref, ga_ref, s0_ref, alog_ref, dtb_ref, ng_ref, bdones_ref,
                ea_ref, eb_ref, bdm_ref, bdmf_ref, lvl_ref,
                o_ref, sout_ref,
                q_s, k_s, v_s, gcx_s, bex_s, o_s, s_ref, *, tt_in, tt):
    t = pl.program_id(1)

    @pl.when(t == 0)
    def _():
        s_ref[...] = s0_ref[0]

    c = qkv_ref[0]
    if tt > tt_in:
        c = jnp.concatenate([c, jnp.zeros((tt - tt_in, QKV_A), F32)], axis=0)
    bdones = bdones_ref[...]
    q = c[:, 0:W_GRP]
    k = c[:, W_GRP:2 * W_GRP]
    ss = _head_mean_square(jnp.concatenate([q, k], axis=0), bdones) * D_HEAD
    q_s[...] = q * lax.rsqrt(ss[:tt] + EPS) * (D_HEAD ** -0.5)
    k_s[...] = k * lax.rsqrt(ss[tt:] + EPS)
    v_s[...] = c[:, 2 * W_GRP:3 * W_GRP]

    ab = ab_ref[0]
    g = -jnp.exp(alog_ref[...]) * _softplus(ab + dtb_ref[...])
    beta = _sigmoid(ab)
    if tt > tt_in:
        pad = jnp.zeros((tt - tt_in, AB_PAD), F32)
        g = jnp.concatenate([g, pad], axis=0)
        beta = jnp.concatenate([beta, pad], axis=0)
    rin = jnp.bitwise_and(lax.broadcasted_iota(jnp.int32, (tt, AB_PAD), 0), CHUNK - 1)
    gc = g
    step = 1
    while step < CHUNK:
        gc = gc + jnp.where(rin >= step, pltpu.roll(gc, step, 0), 0.0)
        step *= 2
    gcx_s[...] = _dot_split(gc, ea_ref[...], 3)
    bex_s[...] = _dot_split(beta, eb_ref[...], 2)

    _gdn_tile(q_s, k_s, v_s, gcx_s, bex_s, o_s, s_ref, bdm_ref, bdmf_ref, lvl_ref, tt // CHUNK)

    o = o_s[0:tt_in, :]
    o_ref[0] = o * lax.rsqrt(_head_mean_square(o, bdones) + EPS) * ng_ref[...] * ga_ref[0]

    @pl.when(t == pl.num_programs(1) - 1)
    def _():
        sout_ref[0] = s_ref[...]


def _gdn_masks():
    idx = np.arange(MXU_W)
    head, pos = idx // D_HEAD, idx % D_HEAD
    bdm = head[:, None] == head[None, :]
    levels = []
    for li in range(N_DOUBLINGS):
        n = 2 << li
        pr, pc = pos[:, None], pos[None, :]
        levels.append(bdm & (pr // (2 * n) == pc // (2 * n)) & (pr // n != pc // n))
    expand = np.arange(W_GRP)[None, :] // D_HEAD == np.arange(AB_PAD)[:, None]
    return dict(bdm=jnp.asarray(bdm, BF16), bdmf=jnp.asarray(bdm, F32),
                lvl=jnp.asarray(np.stack(levels), BF16),
                ea=jnp.asarray(expand, BF16), eb=jnp.asarray(np.roll(expand, N_HEADS, axis=0), BF16))


def _gdn(qkv, ab, ga, s0, alog, dtb, ng, bdones, tt_in):
    b, t, _ = qkv.shape
    tt = -(-tt_in // CHUNK) * CHUNK
    assert t % tt_in == 0 and (tt == tt_in or t == tt_in)
    m = _gdn_masks()
    row = lambda w: pl.BlockSpec((1, tt_in, w), lambda i, j: (i, j, 0))
    const = lambda s: pl.BlockSpec(s, lambda i, j: (0,) * len(s))
    per_b = lambda s: pl.BlockSpec((1,) + s, lambda i, j: (i,) + (0,) * len(s))
    return pl.pallas_call(
        functools.partial(_gdn_kernel, tt_in=tt_in, tt=tt),
        grid=(b, t // tt_in),
        in_specs=[row(QKV_A), row(AB_PAD), row(W_GRP), per_b((D_HEAD, W_GRP)),
                  const((1, AB_PAD)), const((1, AB_PAD)), const((1, W_GRP)),
                  const((W_GRP, W_GRP)), const((AB_PAD, W_GRP)), const((AB_PAD, W_GRP)),
                  const((MXU_W, MXU_W)), const((MXU_W, MXU_W)), const((N_DOUBLINGS, MXU_W, MXU_W))],
        out_specs=[row(W_GRP), per_b((D_HEAD, W_GRP))],
        out_shape=[jax.ShapeDtypeStruct((b, t, W_GRP), F32),
                   jax.ShapeDtypeStruct((b, D_HEAD, W_GRP), F32)],
        scratch_shapes=[pltpu.VMEM((tt, W_GRP), F32)] * 6 + [pltpu.VMEM((D_HEAD, W_GRP), F32)],
        compiler_params=pltpu.CompilerParams(dimension_semantics=("arbitrary", "arbitrary"),
                                             vmem_limit_bytes=V7X_VMEM_LIMIT),
        name="gdn",
    )(qkv, ab, ga, s0, alog, dtb, ng, bdones, m["ea"], m["eb"], m["bdm"], m["bdmf"], m["lvl"])


def _attend_heads(q, kw, vw, bias_ref, first_valid):
    lq, span = q.shape[0], kw.shape[0]
    col = lax.broadcasted_iota(jnp.int32, (lq, span), 1)
    valid = col >= first_valid
    head = lambda h: slice(h * D_HEAD, (h + 1) * D_HEAD)
    scores = lambda h: lax.dot_general(q[:, head(h)], kw[:, head(h)], (((1,), (1,)), ((), ())),
                                       preferred_element_type=F32)
    outs = []
    ahead = 3
    pending = [scores(h) for h in range(ahead)]
    for h in range(N_HEADS):
        s = pending.pop(0)
        if h + ahead < N_HEADS:
            pending.append(scores(h + ahead))
        s = jnp.where(valid, s + bias_ref[h], -jnp.inf)
        m = jnp.max(s, axis=-1, keepdims=True)
        p = jnp.exp(s - m)
        l = jnp.sum(p, axis=-1, keepdims=True)
        o = jnp.dot(p.astype(BF16), vw[:, head(h)], preferred_element_type=F32)
        outs.append(o / l)
    return jnp.concatenate(outs, axis=1)


def _attn_prompt_kernel(q_ref, kp_ref, kc_ref, vp_ref, vc_ref, gb_ref, bias_ref, o_ref, kwin_ref, vwin_ref):
    j = pl.program_id(1)
    tq = BAND_PAST
    kwin_ref[0:tq, :] = kp_ref[0]
    kwin_ref[tq:2 * tq, :] = kc_ref[0]
    vwin_ref[0:tq, :] = vp_ref[0]
    vwin_ref[tq:2 * tq, :] = vc_ref[0]

    def chunk_group(c, carry):
        r0 = pl.multiple_of(c * (Q_CHUNKS * CHUNK), Q_CHUNKS * CHUNK)
        rows = pl.ds(r0, Q_CHUNKS * CHUNK)
        win = pl.ds(r0, BAND_PAST + Q_CHUNKS * CHUNK)
        first_valid = jnp.where(j == 0, BAND_PAST - r0, 0)
        o = _attend_heads(q_ref[0, rows, :], kwin_ref[win, :], vwin_ref[win, :], bias_ref, first_valid)
        o_ref[0, rows, :] = o * gb_ref[0, rows, :]
        return carry

    lax.fori_loop(0, tq // (Q_CHUNKS * CHUNK), chunk_group, 0)


def _attn_prompt(q, k, v, gb, bias):
    b, t, _ = q.shape
    tq = BAND_PAST
    assert t % tq == 0
    cur = pl.BlockSpec((1, tq, W_GRP), lambda i, j: (i, j, 0))
    prev = pl.BlockSpec((1, tq, W_GRP), lambda i, j: (i, jnp.maximum(j - 1, 0), 0))
    return pl.pallas_call(
        _attn_prompt_kernel,
        grid=(b, t // tq),
        in_specs=[cur, prev, cur, prev, cur, cur,
                  pl.BlockSpec(bias.shape, lambda i, j: (0, 0, 0))],
        out_specs=cur,
        out_shape=jax.ShapeDtypeStruct((b, t, W_GRP), F32),
        scratch_shapes=[pltpu.VMEM((2 * tq, W_GRP), BF16), pltpu.VMEM((2 * tq, W_GRP), BF16)],
        compiler_params=pltpu.CompilerParams(dimension_semantics=("arbitrary", "arbitrary"),
                                             vmem_limit_bytes=V7X_VMEM_LIMIT),
        name="attn_prompt",
    )(q, k, k, v, v, gb, bias)


def _attn_sample_kernel(q_ref, kn_ref, vn_ref, ck_ref, cv_ref, gb_ref, bias_ref, o_ref):
    kw = jnp.concatenate([ck_ref[0].astype(BF16), kn_ref[0]], axis=0)
    vw = jnp.concatenate([cv_ref[0].astype(BF16), vn_ref[0]], axis=0)
    o = _attend_heads(q_ref[0], kw, vw, bias_ref, 0)
    o_ref[0] = o * gb_ref[0]


def _attn_sample(q, k, v, ck, cv, gb, bias):
    b, tn, _ = q.shape
    wc = ck.shape[1]
    new = pl.BlockSpec((1, tn, W_GRP), lambda i: (i, 0, 0))
    old = pl.BlockSpec((1, wc, W_GRP), lambda i: (i, 0, 0))
    return pl.pallas_call(
        _attn_sample_kernel,
        grid=(b,),
        in_specs=[new, new, new, old, old, new, pl.BlockSpec((N_HEADS, tn, wc + tn), lambda i: (0, 0, 0))],
        out_specs=new,
        out_shape=jax.ShapeDtypeStruct((b, tn, W_GRP), F32),
        compiler_params=pltpu.CompilerParams(dimension_semantics=("arbitrary",),
                                             vmem_limit_bytes=V7X_VMEM_LIMIT),
        name="attn_sample",
    )(q, k, v, ck, cv, gb, bias)


def _outproj_kernel(x_ref, oa_ref, ob_ref, p_ref, wo_ref, wp_ref, pg_ref, wg_ref, y_ref):
    h = x_ref[0] + (_dot(oa_ref[0], wo_ref[0:W_GRP, :]) + _dot(ob_ref[0], wo_ref[W_GRP:2 * W_GRP, :]))
    e = _dot(p_ref[0], wp_ref[...])
    e = e * lax.rsqrt(jnp.mean(e * e, axis=-1, keepdims=True) + EPS) * pg_ref[...]
    y_ref[0] = h + _sigmoid(_dot(h, wg_ref[...])) * e


def _outproj(x, oa, ob, p, wo, wp, pg, wg, tm):
    b, t, d = x.shape
    pd = p.shape[-1]
    assert t % tm == 0
    row = lambda w: pl.BlockSpec((1, tm, w), lambda i, j: (i, j, 0))
    const = lambda s: pl.BlockSpec(s, lambda i, j: (0,) * len(s))
    return pl.pallas_call(
        _outproj_kernel,
        grid=(b, t // tm),
        in_specs=[row(d), row(W_GRP), row(W_GRP), row(pd), const((2 * W_GRP, d)), const((pd, d)),
                  const((1, d)), const((d, d))],
        out_specs=row(d),
        out_shape=jax.ShapeDtypeStruct((b, t, d), F32),
        compiler_params=pltpu.CompilerParams(dimension_semantics=("arbitrary", "arbitrary"),
                                             vmem_limit_bytes=V7X_VMEM_LIMIT),
        name="outproj",
    )(x, oa, ob, p, wo, wp, pg, wg)


def _rel_bias_table(table, q0, nq, nk):
    n = nk + nq - 1
    m = (q0 + nq - 1) - np.arange(n + 1)
    rev = table[:, np.clip(m, -MAX_REL, MAX_REL) + MAX_REL].astype(F32)
    skew = jnp.tile(rev, (1, nq))[:, :nq * n].reshape(-1, nq, n)
    return skew[:, :, nq - 1:nq - 1 + nk]


def _group_bias(bias):
    h = bias.shape[0]
    rows = []
    for c in range(Q_CHUNKS):
        left = jnp.full((h, CHUNK, c * CHUNK), MASKED, F32)
        right = jnp.full((h, CHUNK, (Q_CHUNKS - 1 - c) * CHUNK), MASKED, F32)
        rows.append(jnp.concatenate([left, bias, right], axis=2))
    return jnp.concatenate(rows, axis=1)


def _pad_lanes(v, width):
    return jnp.pad(v.reshape(1, -1), ((0, 0), (0, width - v.shape[-1])))


def _layer_weights(ln_g, w_in, conv_w, a_log, dt_bias, gdn_norm_g, q_norm_g, k_norm_g, w_out, w_ple_proj,
                   ple_norm_g, w_ple_gate):
    ab0 = QKV_A
    ab1 = QKV_A + 2 * N_HEADS
    w_r = jnp.concatenate([w_in[:, :ab0], w_in[:, ab1:],
                           jnp.pad(w_in[:, ab0:ab1], ((0, 0), (0, AB_PAD - 2 * N_HEADS)))], axis=1)
    head = jnp.arange(W_GRP) // D_HEAD
    return dict(
        ln_g=ln_g.reshape(1, -1), w_r=w_r.astype(BF16),
        qg=jnp.tile(q_norm_g, N_HEADS).reshape(1, -1), kg=jnp.tile(k_norm_g, N_HEADS).reshape(1, -1),
        bd=(head[:, None] == head[None, :]).astype(BF16),
        cw=conv_w, alog=_pad_lanes(a_log, AB_PAD), dtb=_pad_lanes(dt_bias, AB_PAD),
        ng=jnp.tile(gdn_norm_g, N_HEADS).reshape(1, -1),
        wo=w_out.astype(BF16), wp=w_ple_proj.astype(BF16), pg=ple_norm_g.reshape(1, -1),
        wg=w_ple_gate.astype(BF16))


def _state_to_lanes(s):
    b = s.shape[0]
    return s.transpose(0, 2, 1, 3).reshape(b, D_HEAD, W_GRP)


def _state_from_lanes(s):
    b = s.shape[0]
    return s.reshape(b, D_HEAD, N_HEADS, D_HEAD).transpose(0, 2, 1, 3)


def _layer(h, p_i, conv0, s0, caches, bias, w, tm, tt_in):
    qkv, ab, ga, q, k, v, gb, kst, vst, conv_new = _inproj(h, w["ln_g"], w["w_r"], w["qg"], w["kg"], w["bd"],
                                                           conv0, w["cw"], tm)
    oa, s_new = _gdn(qkv, ab, ga, _state_to_lanes(s0), w["alog"], w["dtb"], w["ng"], w["bd"], tt_in)
    if caches is None:
        ob = _attn_prompt(q, k, v, gb, bias)
        rows, tm_out = (lambda a: a), tm
    else:
        ob = _attn_sample(q, k, v, caches[0], caches[1], gb, bias)
        rows, tm_out = (lambda a: a.reshape(1, -1, a.shape[-1])), h.shape[0] * h.shape[1]
    y = _outproj(rows(h), rows(oa), rows(ob), rows(p_i), w["wo"], w["wp"], w["pg"], w["wg"], tm_out)
    return y.reshape(h.shape), conv_new[:, CONV_ROWS - (CONV_W - 1):], _state_from_lanes(s_new), kst, vst


def kernel(x_prompt, x_sample, state_conv, state_gdn, cache_k, cache_v, p_prompt, p_sample, ln_g, w_in, conv_w, a_log, dt_bias, gdn_norm_g, q_norm_g, k_norm_g, rel_bias, w_out, w_ple_proj, ple_norm_g, w_ple_gate):
    depth = ln_g.shape[0]
    bp, tp, d = x_prompt.shape
    bs, ts, _ = x_sample.shape
    wc = cache_k.shape[2]
    assert tp % BAND_PAST == 0 and ts <= CHUNK and ts % CONV_ROWS == 0
    hp = x_prompt
    hs = x_sample
    outs = [[] for _ in range(8)]
    for i in range(depth):
        w = _layer_weights(ln_g[i], w_in[i], conv_w[i], a_log[i], dt_bias[i], gdn_norm_g[i], q_norm_g[i],
                           k_norm_g[i], w_out[i], w_ple_proj[i], ple_norm_g[i], w_ple_gate[i])
        conv0 = jnp.zeros((bp, CONV_ROWS, QKV_A), F32)
        s0 = jnp.zeros((bp, N_HEADS, D_HEAD, D_HEAD), F32)
        bias_p = _group_bias(_rel_bias_table(rel_bias[i], BAND_PAST, CHUNK, BAND_PAST + CHUNK))
        hp, c_p, g_p, k_p, v_p = _layer(hp, p_prompt[i], conv0, s0, None, bias_p, w,
                                        tm=BAND_PAST, tt_in=4 * CHUNK)
        conv0_s = jnp.pad(state_conv[i], ((0, 0), (CONV_ROWS - (CONV_W - 1), 0), (0, 0)))
        caches = (cache_k[i].reshape(bs, wc, W_GRP), cache_v[i].reshape(bs, wc, W_GRP))
        bias_s = _rel_bias_table(rel_bias[i], wc, ts, wc + ts)
        hs, c_s, g_s, k_s, v_s = _layer(hs, p_sample[i], conv0_s, state_gdn[i],
                                        caches, bias_s, w, tm=ts, tt_in=ts)
        new = (c_p, g_p, k_p.reshape(bp, BAND_PAST, N_HEADS, D_HEAD), v_p.reshape(bp, BAND_PAST, N_HEADS, D_HEAD),
               c_s, g_s, k_s.reshape(bs, ts, N_HEADS, D_HEAD), v_s.reshape(bs, ts, N_HEADS, D_HEAD))
        for lst, a in zip(outs, new):
            lst.append(a)
    stack = lambda lst: lst[0][None] if len(lst) == 1 else jnp.stack(lst)
    return (hp, hs) + tuple(stack(lst) for lst in outs)
```

```python
import functools

import numpy as np

import jax
import jax.numpy as jnp
from jax import lax
from jax.experimental import pallas as pl
from jax.experimental.pallas import tpu as pltpu

F32 = jnp.float32
BF16 = jnp.bfloat16
EPS = 1e-6

CHUNK = 64
N_HEADS = 8
D_HEAD = 64
W_GRP = N_HEADS * D_HEAD
QKV_A = 3 * W_GRP
CONV_W = 4
BAND_PAST = 8 * CHUNK
MAX_REL = 128
AB_PAD = 128
CONV_ROWS = 8
V7X_VMEM_LIMIT = 52 * 1024 * 1024
MXU_W = 256
MXU_HEADS = MXU_W // D_HEAD
N_DOUBLINGS = 5
Q_CHUNKS = 8
MASKED = -1e30

_OFF_ZA = QKV_A
_OFF_QB = _OFF_ZA + W_GRP
_OFF_KB = _OFF_QB + W_GRP
_OFF_VB = _OFF_KB + W_GRP
_OFF_ZB = _OFF_VB + W_GRP
_OFF_AB = _OFF_ZB + W_GRP
_D_PROJ_R = _OFF_AB + AB_PAD


def _dot(a, b):
    return jnp.dot(a.astype(BF16), b.astype(BF16), preferred_element_type=F32)


def _dot_split(x, w_bf16, n_terms):
    terms, r = [], x
    for _ in range(n_terms):
        t = r.astype(BF16)
        terms.append(t)
        r = r - t.astype(F32)
    acc = None
    for t in reversed(terms):
        d = jnp.dot(t, w_bf16, preferred_element_type=F32)
        acc = d if acc is None else acc + d
    return acc


def _sigmoid(x):
    return 1.0 / (1.0 + jnp.exp(-x))


def _silu(x):
    return x * _sigmoid(x)


def _softplus(x):
    return jnp.maximum(x, 0.0) + jnp.log1p(jnp.exp(-jnp.abs(x)))


def _head_mean_square(y, blockdiag_ones):
    return jnp.dot((y * y).astype(BF16), blockdiag_ones, preferred_element_type=F32) * (1.0 / D_HEAD)


def _inproj_kernel(x_ref, lng_ref, w_ref, qg_ref, kg_ref, bd_ref,
                   qkv_ref, ab_ref, ga_ref, q_ref, k_ref, v_ref, gb_ref, kst_ref, vst_ref):
    x = x_ref[0]
    xn = x * lax.rsqrt(jnp.mean(x * x, axis=-1, keepdims=True) + EPS) * lng_ref[...]
    xb = xn.astype(BF16)

    def proj(lo, hi):
        return jnp.dot(xb, w_ref[:, lo:hi], preferred_element_type=F32)

    bd = bd_ref[...]
    qkv_ref[0] = proj(0, QKV_A)
    ga_ref[0] = _silu(proj(_OFF_ZA, _OFF_QB))
    qb = proj(_OFF_QB, _OFF_KB)
    kb = proj(_OFF_KB, _OFF_VB)
    ms = _head_mean_square(jnp.concatenate([qb, kb], axis=0), bd)
    qn = qb * lax.rsqrt(ms[:qb.shape[0]] + EPS) * qg_ref[...]
    q_ref[0] = (qn * (D_HEAD ** -0.5)).astype(BF16)
    kn = kb * lax.rsqrt(ms[qb.shape[0]:] + EPS) * kg_ref[...]
    k_ref[0] = kn.astype(BF16)
    kst_ref[0] = kn
    vb = proj(_OFF_VB, _OFF_ZB)
    v_ref[0] = vb.astype(BF16)
    vst_ref[0] = vb
    gb_ref[0] = _silu(proj(_OFF_ZB, _OFF_AB))
    ab_ref[0] = proj(_OFF_AB, _D_PROJ_R)


def _inproj(x, ln_g, w_r, qg, kg, bd, tm):
    b, t, d = x.shape
    assert t % tm == 0
    grid = (b, t // tm)
    row = lambda w: pl.BlockSpec((1, tm, w), lambda i, j: (i, j, 0))
    const = lambda s: pl.BlockSpec(s, lambda i, j: (0,) * len(s))
    last = pl.BlockSpec((1, tm, W_GRP), lambda i, j: (i, 0, 0))
    sds = lambda w, dt: jax.ShapeDtypeStruct((b, t, w), dt)
    return pl.pallas_call(
        _inproj_kernel,
        grid=grid,
        in_specs=[row(d), const((1, d)), const((d, _D_PROJ_R)), const((1, W_GRP)), const((1, W_GRP)),
                  const((W_GRP, W_GRP))],
        out_specs=[row(QKV_A), row(AB_PAD), row(W_GRP), row(W_GRP), row(W_GRP), row(W_GRP), row(W_GRP),
                   last, last],
        out_shape=[sds(QKV_A, F32), sds(AB_PAD, F32), sds(W_GRP, F32), sds(W_GRP, BF16), sds(W_GRP, BF16),
                   sds(W_GRP, BF16), sds(W_GRP, F32),
                   jax.ShapeDtypeStruct((b, tm, W_GRP), F32), jax.ShapeDtypeStruct((b, tm, W_GRP), F32)],
        compiler_params=pltpu.CompilerParams(dimension_semantics=("arbitrary", "arbitrary"),
                                             vmem_limit_bytes=V7X_VMEM_LIMIT),
        name="inproj",
    )(x, ln_g, w_r, qg, kg, bd)


def _tile_rows(x, n):
    return jnp.concatenate([x] * n, axis=0)


def _gdn_tile(q_s, k_s, v_s, gcx_s, bex_s, o_s, s_ref, bdm_ref, bdmf_ref, lvl_ref, n_chunks):
    row = lax.broadcasted_iota(jnp.int32, (CHUNK, MXU_W), 0)
    col = jnp.bitwise_and(lax.broadcasted_iota(jnp.int32, (CHUNK, MXU_W), 1), D_HEAD - 1)
    incl = row >= col
    strict = row > col
    diag = row == col
    pair = strict & (jnp.right_shift(row, 1) == jnp.right_shift(col, 1))
    bdm = bdm_ref[...]
    blockdiag = lambda x: _tile_rows(x.astype(BF16), MXU_HEADS) * bdm
    mm = lambda a, b: jnp.dot(a, b, preferred_element_type=F32)

    chains = []
    for c in range(n_chunks):
        rows = slice(c * CHUNK, (c + 1) * CHUNK)
        for grp in range(N_HEADS // MXU_HEADS):
            ls = slice(grp * MXU_W, (grp + 1) * MXU_W)
            qc, kc, vc, gcx, bex = (r[rows, ls] for r in (q_s, k_s, v_s, gcx_s, bex_s))
            gc_row = jnp.sum(jnp.where(diag, gcx, 0.0), axis=0, keepdims=True)
            gc_last = gcx[CHUNK - 1:CHUNK, :]
            egc = jnp.exp(gcx)
            k16 = kc.astype(BF16)
            chains.append(dict(
                rows=rows, ls=ls, bex=bex, k16=k16,
                decay=jnp.where(incl, jnp.exp(jnp.where(incl, gcx - gc_row, 0.0)), 0.0),
                g_tot=jnp.exp(gc_last), u_rhs=vc * bex, w_rhs=kc * (bex * egc),
                qk_lhs=jnp.concatenate([qc.astype(BF16), k16], axis=0),
                q_dec=(qc * egc).astype(BF16), k_tail=(kc * jnp.exp(gc_last - gcx)).astype(BF16)))
    for ch in chains:
        ch["sc"] = lax.dot_general(ch["qk_lhs"], blockdiag(ch["k16"]), (((1,), (1,)), ((), ())),
                                   preferred_element_type=F32)
    for ch in chains:
        sc = ch.pop("sc")
        ch["qk"] = (sc[0:CHUNK] * ch["decay"]).astype(BF16)
        a_low = jnp.where(strict, ch["bex"] * sc[CHUNK:2 * CHUNK] * ch["decay"], 0.0)
        ch["a_rows"] = _tile_rows(a_low.astype(BF16), MXU_HEADS)
        ch["dinv"] = jnp.where(diag, 1.0, 0.0) - jnp.where(pair, a_low, 0.0)
    for li in range(N_DOUBLINGS):
        for ch in chains:
            ch["t1"] = mm(ch["dinv"].astype(BF16), ch["a_rows"] * lvl_ref[li])
        for ch in chains:
            ch["dinv"] = ch["dinv"] - mm(ch.pop("t1").astype(BF16), blockdiag(ch["dinv"]))
    for ch in chains:
        d16 = ch["dinv"].astype(BF16)
        ch["u"] = mm(d16, blockdiag(ch["u_rhs"]))
        ch["w"] = mm(d16, blockdiag(ch["w_rhs"]))
    for ch in chains:
        rows, ls = ch["rows"], ch["ls"]
        s = s_ref[:, ls]
        wq = mm(jnp.concatenate([ch["w"].astype(BF16), ch["q_dec"]], axis=0), blockdiag(s))
        v16 = (ch["u"] - wq[0:CHUNK]).astype(BF16)
        o_s[rows, ls] = wq[CHUNK:2 * CHUNK] + mm(ch["qk"], _tile_rows(v16, MXU_HEADS) * bdm)
        r = lax.dot_general(ch["k_tail"], v16, (((0,), (0,)), ((), ())),
                            preferred_element_type=F32) * bdmf_ref[...]
        s_ref[:, ls] = s * ch["g_tot"] + ((r[0:D_HEAD] + r[D_HEAD:2 * D_HEAD])
                                          + (r[2 * D_HEAD:3 * D_HEAD] + r[3 * D_HEAD:4 * D_HEAD]))


def _gdn_kernel(qkv_ref, ab_ref, ga_ref, conv0_ref, s0_ref, cw_ref, alog_ref, dtb_ref, ng_ref, bdones_ref,
                ea_ref, eb_ref, bdm_ref, bdmf_ref, lvl_ref,
                o_ref, convout_ref, sout_ref,
                xcat_ref, q_s, k_s, v_s, gcx_s, bex_s, o_s, s_ref, *, tt_in, tt):
    t = pl.program_id(1)

    @pl.when(t == 0)
    def _():
        xcat_ref[0:CONV_ROWS, :] = conv0_ref[0]
        s_ref[...] = s0_ref[0]

    xcat_ref[CONV_ROWS:CONV_ROWS + tt_in, :] = qkv_ref[0]
    if tt > tt_in:
        xcat_ref[CONV_ROWS + tt_in:CONV_ROWS + tt, :] = jnp.zeros((tt - tt_in, QKV_A), F32)
    cw = cw_ref[...]
    first = CONV_ROWS - (CONV_W - 1)
    conv = xcat_ref[first:first + tt, :] * cw[0:1, :]
    for j in range(1, CONV_W):
        conv = conv + xcat_ref[first + j:first + j + tt, :] * cw[j:j + 1, :]
    carry = xcat_ref[tt_in:tt_in + CONV_ROWS, :]
    convout_ref[0] = carry
    xcat_ref[0:CONV_ROWS, :] = carry

    c = _silu(conv)
    bdones = bdones_ref[...]
    q = c[:, 0:W_GRP]
    k = c[:, W_GRP:2 * W_GRP]
    ss = _head_mean_square(jnp.concatenate([q, k], axis=0), bdones) * D_HEAD
    q_s[...] = q * lax.rsqrt(ss[:tt] + EPS) * (D_HEAD ** -0.5)
    k_s[...] = k * lax.rsqrt(ss[tt:] + EPS)
    v_s[...] = c[:, 2 * W_GRP:3 * W_GRP]

    ab = ab_ref[0]
    g = -jnp.exp(alog_ref[...]) * _softplus(ab + dtb_ref[...])
    beta = _sigmoid(ab)
    if tt > tt_in:
        pad = jnp.zeros((tt - tt_in, AB_PAD), F32)
        g = jnp.concatenate([g, pad], axis=0)
        beta = jnp.concatenate([beta, pad], axis=0)
    rin = jnp.bitwise_and(lax.broadcasted_iota(jnp.int32, (tt, AB_PAD), 0), CHUNK - 1)
    gc = g
    step = 1
    while step < CHUNK:
        gc = gc + jnp.where(rin >= step, pltpu.roll(gc, step, 0), 0.0)
        step *= 2
    gcx_s[...] = _dot_split(gc, ea_ref[...], 3)
    bex_s[...] = _dot_split(beta, eb_ref[...], 2)

    _gdn_tile(q_s, k_s, v_s, gcx_s, bex_s, o_s, s_ref, bdm_ref, bdmf_ref, lvl_ref, tt // CHUNK)

    o = o_s[0:tt_in, :]
    o_ref[0] = o * lax.rsqrt(_head_mean_square(o, bdones) + EPS) * ng_ref[...] * ga_ref[0]

    @pl.when(t == pl.num_programs(1) - 1)
    def _():
        sout_ref[0] = s_ref[...]


def _gdn_masks():
    idx = np.arange(MXU_W)
    head, pos = idx // D_HEAD, idx % D_HEAD
    bdm = head[:, None] == head[None, :]
    levels = []
    for li in range(N_DOUBLINGS):
        n = 2 << li
        pr, pc = pos[:, None], pos[None, :]
        levels.append(bdm & (pr // (2 * n) == pc // (2 * n)) & (pr // n != pc // n))
    expand = np.arange(W_GRP)[None, :] // D_HEAD == np.arange(AB_PAD)[:, None]
    return dict(bdm=jnp.asarray(bdm, BF16), bdmf=jnp.asarray(bdm, F32),
                lvl=jnp.asarray(np.stack(levels), BF16),
                ea=jnp.asarray(expand, BF16), eb=jnp.asarray(np.roll(expand, N_HEADS, axis=0), BF16))


def _gdn(qkv, ab, ga, conv0, s0, cw, alog, dtb, ng, bdones, tt_in):
    b, t, _ = qkv.shape
    tt = -(-tt_in // CHUNK) * CHUNK
    assert t % tt_in == 0 and tt_in >= CONV_ROWS and (tt == tt_in or t == tt_in)
    m = _gdn_masks()
    row = lambda w: pl.BlockSpec((1, tt_in, w), lambda i, j: (i, j, 0))
    const = lambda s: pl.BlockSpec(s, lambda i, j: (0,) * len(s))
    per_b = lambda s: pl.BlockSpec((1,) + s, lambda i, j: (i,) + (0,) * len(s))
    return pl.pallas_call(
        functools.partial(_gdn_kernel, tt_in=tt_in, tt=tt),
        grid=(b, t // tt_in),
        in_specs=[row(QKV_A), row(AB_PAD), row(W_GRP), per_b((CONV_ROWS, QKV_A)), per_b((D_HEAD, W_GRP)),
                  const((CONV_W, QKV_A)), const((1, AB_PAD)), const((1, AB_PAD)), const((1, W_GRP)),
                  const((W_GRP, W_GRP)), const((AB_PAD, W_GRP)), const((AB_PAD, W_GRP)),
                  const((MXU_W, MXU_W)), const((MXU_W, MXU_W)), const((N_DOUBLINGS, MXU_W, MXU_W))],
        out_specs=[row(W_GRP), per_b((CONV_ROWS, QKV_A)), per_b((D_HEAD, W_GRP))],
        out_shape=[jax.ShapeDtypeStruct((b, t, W_GRP), F32),
                   jax.ShapeDtypeStruct((b, CONV_ROWS, QKV_A), F32),
                   jax.ShapeDtypeStruct((b, D_HEAD, W_GRP), F32)],
        scratch_shapes=[pltpu.VMEM((tt + CONV_ROWS, QKV_A), F32)]
        + [pltpu.VMEM((tt, W_GRP), F32)] * 6 + [pltpu.VMEM((D_HEAD, W_GRP), F32)],
        compiler_params=pltpu.CompilerParams(dimension_semantics=("arbitrary", "arbitrary"),
                                             vmem_limit_bytes=V7X_VMEM_LIMIT),
        name="gdn",
    )(qkv, ab, ga, conv0, s0, cw, alog, dtb, ng, bdones, m["ea"], m["eb"], m["bdm"], m["bdmf"], m["lvl"])


def _attend_heads(q, kw, vw, bias_ref, first_valid):
    lq, span = q.shape[0], kw.shape[0]
    col = lax.broadcasted_iota(jnp.int32, (lq, span), 1)
    valid = col >= first_valid
    head = lambda h: slice(h * D_HEAD, (h + 1) * D_HEAD)
    scores = lambda h: lax.dot_general(q[:, head(h)], kw[:, head(h)], (((1,), (1,)), ((), ())),
                                       preferred_element_type=F32)
    outs = []
    ahead = 3
    pending = [scores(h) for h in range(ahead)]
    for h in range(N_HEADS):
        s = pending.pop(0)
        if h + ahead < N_HEADS:
            pending.append(scores(h + ahead))
        s = jnp.where(valid, s + bias_ref[h], -jnp.inf)
        m = jnp.max(s, axis=-1, keepdims=True)
        p = jnp.exp(s - m)
        l = jnp.sum(p, axis=-1, keepdims=True)
        o = jnp.dot(p.astype(BF16), vw[:, head(h)], preferred_element_type=F32)
        outs.append(o / l)
    return jnp.concatenate(outs, axis=1)


def _attn_prompt_kernel(q_ref, kp_ref, kc_ref, vp_ref, vc_ref, gb_ref, bias_ref, o_ref, kwin_ref, vwin_ref):
    j = pl.program_id(1)
    tq = BAND_PAST
    kwin_ref[0:tq, :] = kp_ref[0]
    kwin_ref[tq:2 * tq, :] = kc_ref[0]
    vwin_ref[0:tq, :] = vp_ref[0]
    vwin_ref[tq:2 * tq, :] = vc_ref[0]

    def chunk_group(c, carry):
        r0 = pl.multiple_of(c * (Q_CHUNKS * CHUNK), Q_CHUNKS * CHUNK)
        rows = pl.ds(r0, Q_CHUNKS * CHUNK)
        win = pl.ds(r0, BAND_PAST + Q_CHUNKS * CHUNK)
        first_valid = jnp.where(j == 0, BAND_PAST - r0, 0)
        o = _attend_heads(q_ref[0, rows, :], kwin_ref[win, :], vwin_ref[win, :], bias_ref, first_valid)
        o_ref[0, rows, :] = o * gb_ref[0, rows, :]
        return carry

    lax.fori_loop(0, tq // (Q_CHUNKS * CHUNK), chunk_group, 0)


def _attn_prompt(q, k, v, gb, bias):
    b, t, _ = q.shape
    tq = BAND_PAST
    assert t % tq == 0
    cur = pl.BlockSpec((1, tq, W_GRP), lambda i, j: (i, j, 0))
    prev = pl.BlockSpec((1, tq, W_GRP), lambda i, j: (i, jnp.maximum(j - 1, 0), 0))
    return pl.pallas_call(
        _attn_prompt_kernel,
        grid=(b, t // tq),
        in_specs=[cur, prev, cur, prev, cur, cur,
                  pl.BlockSpec(bias.shape, lambda i, j: (0, 0, 0))],
        out_specs=cur,
        out_shape=jax.ShapeDtypeStruct((b, t, W_GRP), F32),
        scratch_shapes=[pltpu.VMEM((2 * tq, W_GRP), BF16), pltpu.VMEM((2 * tq, W_GRP), BF16)],
        compiler_params=pltpu.CompilerParams(dimension_semantics=("arbitrary", "arbitrary"),
                                             vmem_limit_bytes=V7X_VMEM_LIMIT),
        name="attn_prompt",
    )(q, k, k, v, v, gb, bias)


def _attn_sample_kernel(q_ref, kn_ref, vn_ref, ck_ref, cv_ref, gb_ref, bias_ref, o_ref):
    kw = jnp.concatenate([ck_ref[0].astype(BF16), kn_ref[0]], axis=0)
    vw = jnp.concatenate([cv_ref[0].astype(BF16), vn_ref[0]], axis=0)
    o = _attend_heads(q_ref[0], kw, vw, bias_ref, 0)
    o_ref[0] = o * gb_ref[0]


def _attn_sample(q, k, v, ck, cv, gb, bias):
    b, tn, _ = q.shape
    wc = ck.shape[1]
    new = pl.BlockSpec((1, tn, W_GRP), lambda i: (i, 0, 0))
    old = pl.BlockSpec((1, wc, W_GRP), lambda i: (i, 0, 0))
    return pl.pallas_call(
        _attn_sample_kernel,
        grid=(b,),
        in_specs=[new, new, new, old, old, new, pl.BlockSpec((N_HEADS, tn, wc + tn), lambda i: (0, 0, 0))],
        out_specs=new,
        out_shape=jax.ShapeDtypeStruct((b, tn, W_GRP), F32),
        compiler_params=pltpu.CompilerParams(dimension_semantics=("arbitrary",),
                                             vmem_limit_bytes=V7X_VMEM_LIMIT),
        name="attn_sample",
    )(q, k, v, ck, cv, gb, bias)


def _outproj_kernel(x_ref, oa_ref, ob_ref, p_ref, wo_ref, wp_ref, pg_ref, wg_ref, y_ref):
    h = x_ref[0] + (_dot(oa_ref[0], wo_ref[0:W_GRP, :]) + _dot(ob_ref[0], wo_ref[W_GRP:2 * W_GRP, :]))
    e = _dot(p_ref[0], wp_ref[...])
    e = e * lax.rsqrt(jnp.mean(e * e, axis=-1, keepdims=True) + EPS) * pg_ref[...]
    y_ref[0] = h + _sigmoid(_dot(h, wg_ref[...])) * e


def _outproj(x, oa, ob, p, wo, wp, pg, wg, tm):
    b, t, d = x.shape
    pd = p.shape[-1]
    assert t % tm == 0
    row = lambda w: pl.BlockSpec((1, tm, w), lambda i, j: (i, j, 0))
    const = lambda s: pl.BlockSpec(s, lambda i, j: (0,) * len(s))
    return pl.pallas_call(
        _outproj_kernel,
        grid=(b, t // tm),
        in_specs=[row(d), row(W_GRP), row(W_GRP), row(pd), const((2 * W_GRP, d)), const((pd, d)),
                  const((1, d)), const((d, d))],
        out_specs=row(d),
        out_shape=jax.ShapeDtypeStruct((b, t, d), F32),
        compiler_params=pltpu.CompilerParams(dimension_semantics=("arbitrary", "arbitrary"),
                                             vmem_limit_bytes=V7X_VMEM_LIMIT),
        name="outproj",
    )(x, oa, ob, p, wo, wp, pg, wg)


def _rel_bias_table(table, q0, nq, nk):
    n = nk + nq - 1
    m = (q0 + nq - 1) - np.arange(n + 1)
    rev = table[:, np.clip(m, -MAX_REL, MAX_REL) + MAX_REL].astype(F32)
    skew = jnp.tile(rev, (1, nq))[:, :nq * n].reshape(-1, nq, n)
    return skew[:, :, nq - 1:nq - 1 + nk]


def _group_bias(bias):
    h = bias.shape[0]
    rows = []
    for c in range(Q_CHUNKS):
        left = jnp.full((h, CHUNK, c * CHUNK), MASKED, F32)
        right = jnp.full((h, CHUNK, (Q_CHUNKS - 1 - c) * CHUNK), MASKED, F32)
        rows.append(jnp.concatenate([left, bias, right], axis=2))
    return jnp.concatenate(rows, axis=1)


def _pad_lanes(v, width):
    return jnp.pad(v.reshape(1, -1), ((0, 0), (0, width - v.shape[-1])))


def _layer_weights(ln_g, w_in, conv_w, a_log, dt_bias, gdn_norm_g, q_norm_g, k_norm_g, w_out, w_ple_proj,
                   ple_norm_g, w_ple_gate):
    ab0 = QKV_A
    ab1 = QKV_A + 2 * N_HEADS
    w_r = jnp.concatenate([w_in[:, :ab0], w_in[:, ab1:],
                           jnp.pad(w_in[:, ab0:ab1], ((0, 0), (0, AB_PAD - 2 * N_HEADS)))], axis=1)
    head = jnp.arange(W_GRP) // D_HEAD
    return dict(
        ln_g=ln_g.reshape(1, -1), w_r=w_r.astype(BF16),
        qg=jnp.tile(q_norm_g, N_HEADS).reshape(1, -1), kg=jnp.tile(k_norm_g, N_HEADS).reshape(1, -1),
        bd=(head[:, None] == head[None, :]).astype(BF16),
        cw=conv_w, alog=_pad_lanes(a_log, AB_PAD), dtb=_pad_lanes(dt_bias, AB_PAD),
        ng=jnp.tile(gdn_norm_g, N_HEADS).reshape(1, -1),
        wo=w_out.astype(BF16), wp=w_ple_proj.astype(BF16), pg=ple_norm_g.reshape(1, -1),
        wg=w_ple_gate.astype(BF16))


def _state_to_lanes(s):
    b = s.shape[0]
    return s.transpose(0, 2, 1, 3).reshape(b, D_HEAD, W_GRP)


def _state_from_lanes(s):
    b = s.shape[0]
    return s.reshape(b, D_HEAD, N_HEADS, D_HEAD).transpose(0, 2, 1, 3)


def _layer(h, p_i, conv0, s0, caches, bias, w, tm, tt_in):
    qkv, ab, ga, q, k, v, gb, kst, vst = _inproj(h, w["ln_g"], w["w_r"], w["qg"], w["kg"], w["bd"], tm)
    if caches is not None:
        nb = caches[0].shape[0]
        regroup = lambda a: a.reshape(nb, a.shape[1] // nb, a.shape[-1])
        qkv, ab, ga, q, k, v, gb = map(regroup, (qkv, ab, ga, q, k, v, gb))
    oa, conv_new, s_new = _gdn(qkv, ab, ga, conv0, _state_to_lanes(s0), w["cw"], w["alog"], w["dtb"], w["ng"],
                               w["bd"], tt_in)
    if caches is None:
        ob = _attn_prompt(q, k, v, gb, bias)
    else:
        ob = _attn_sample(q, k, v, caches[0], caches[1], gb, bias)
        oa, ob = oa.reshape(h.shape[0], h.shape[1], W_GRP), ob.reshape(h.shape[0], h.shape[1], W_GRP)
    y = _outproj(h, oa, ob, p_i, w["wo"], w["wp"], w["pg"], w["wg"], tm)
    return y, conv_new[:, CONV_ROWS - (CONV_W - 1):], _state_from_lanes(s_new), kst, vst


def kernel(x_prompt, x_sample, state_conv, state_gdn, cache_k, cache_v, p_prompt, p_sample, ln_g, w_in, conv_w, a_log, dt_bias, gdn_norm_g, q_norm_g, k_norm_g, rel_bias, w_out, w_ple_proj, ple_norm_g, w_ple_gate):
    depth = ln_g.shape[0]
    bp, tp, d = x_prompt.shape
    bs, ts, _ = x_sample.shape
    wc = cache_k.shape[2]
    assert tp % BAND_PAST == 0 and ts <= CHUNK and ts % CONV_ROWS == 0
    hp = x_prompt
    hs = x_sample.reshape(1, bs * ts, d)
    outs = [[] for _ in range(8)]
    for i in range(depth):
        w = _layer_weights(ln_g[i], w_in[i], conv_w[i], a_log[i], dt_bias[i], gdn_norm_g[i], q_norm_g[i],
                           k_norm_g[i], w_out[i], w_ple_proj[i], ple_norm_g[i], w_ple_gate[i])
        conv0 = jnp.zeros((bp, CONV_ROWS, QKV_A), F32)
        s0 = jnp.zeros((bp, N_HEADS, D_HEAD, D_HEAD), F32)
        bias_p = _group_bias(_rel_bias_table(rel_bias[i], BAND_PAST, CHUNK, BAND_PAST + CHUNK))
        hp, c_p, g_p, k_p, v_p = _layer(hp, p_prompt[i], conv0, s0, None, bias_p, w,
                                        tm=BAND_PAST, tt_in=8 * CHUNK)
        conv0_s = jnp.pad(state_conv[i], ((0, 0), (CONV_ROWS - (CONV_W - 1), 0), (0, 0)))
        caches = (cache_k[i].reshape(bs, wc, W_GRP), cache_v[i].reshape(bs, wc, W_GRP))
        bias_s = _rel_bias_table(rel_bias[i], wc, ts, wc + ts)
        hs, c_s, g_s, k_s, v_s = _layer(hs, p_sample[i].reshape(1, bs * ts, -1), conv0_s, state_gdn[i],
                                        caches, bias_s, w, tm=bs * ts, tt_in=ts)
        new = (c_p, g_p, k_p.reshape(bp, BAND_PAST, N_HEADS, D_HEAD), v_p.reshape(bp, BAND_PAST, N_HEADS, D_HEAD),
               c_s, g_s, k_s.reshape(bs, ts, N_HEADS, D_HEAD), v_s.reshape(bs, ts, N_HEADS, D_HEAD))
        for lst, a in zip(outs, new):
            lst.append(a)
    stack = lambda lst: lst[0][None] if len(lst) == 1 else jnp.stack(lst)
    return (hp, hs.reshape(bs, ts, d)) + tuple(stack(lst) for lst in outs)
```
